```python
import jax, jax.numpy as jnp
from jax import lax
import numpy as np

D_MODEL = 2048
BATCH = 4
SEQ = 4096
DEPTH = 2
DEC_BATCH = 1
DEC_SEQ = 16384
PAST_LEN = 128

N_HEADS = 16
N_KV_HEADS = 8
HEAD_DIM = D_MODEL // N_HEADS
KV_GROUP = N_HEADS // N_KV_HEADS
Q_BLOCK = 128
ROPE_THETA = 10000.0
GRID_W = 64
AXIS_DIM = HEAD_DIM // 2
QKV_DIM = (N_HEADS + 2 * N_KV_HEADS) * HEAD_DIM
D_RNN = D_MODEL
N_RNN_BLOCKS = 16
RNN_BLOCK = D_RNN // N_RNN_BLOCKS
REC_CONV_W = 4
REC_CONV_LEFT = 2
LRU_C = 8.0
D_FF = 5632
FFN_CONV_W = 3
N_ATTN = (DEPTH + 1) // 2
N_REC = DEPTH // 2
NORM_EPS = 1e-6

kernel_name = 'hybrid_attn_rglru_encoder'


def rmsnorm(x, g):
    xf = x.astype(jnp.float32)
    y = xf * lax.rsqrt(jnp.mean(xf * xf, axis=-1, keepdims=True) + NORM_EPS)
    return (y * g.astype(jnp.float32)).astype(x.dtype)


def dwconv(x, w, b, pad_left):
    k = w.shape[0]
    c = x.shape[-1]
    y = lax.conv_general_dilated(
        x, w[:, None, :].astype(x.dtype), window_strides=(1,),
        padding=[(pad_left, k - 1 - pad_left)],
        dimension_numbers=('NWC', 'WIO', 'NWC'), feature_group_count=c)
    return y + b.astype(x.dtype)


def grid_positions(s):
    rows = s // GRID_W
    row = jnp.repeat(jnp.arange(rows, dtype=jnp.float32), GRID_W)
    col = jnp.tile(jnp.arange(GRID_W, dtype=jnp.float32), rows)
    return row, col


def rope_axis(x, pos):
    inv = ROPE_THETA ** (-jnp.arange(0, AXIS_DIM, 2, dtype=jnp.float32) / AXIS_DIM)
    ang = pos[:, None] * inv[None, :]
    cos = jnp.cos(ang)[None, :, None, :]
    sin = jnp.sin(ang)[None, :, None, :]
    x1, x2 = jnp.split(x, 2, axis=-1)
    return jnp.concatenate([x1 * cos - x2 * sin, x2 * cos + x1 * sin], axis=-1)


def axial_rope(x, row, col):
    xf = x.astype(jnp.float32)
    xr, xc = jnp.split(xf, 2, axis=-1)
    return jnp.concatenate([rope_axis(xr, row), rope_axis(xc, col)], axis=-1).astype(x.dtype)


def attention_mixer(h, w_qkv, q_gain, k_gain, w_o):
    b, s, _ = h.shape
    qkv = h @ w_qkv.astype(h.dtype)
    q, k, v = jnp.split(qkv, [N_HEADS * HEAD_DIM, (N_HEADS + N_KV_HEADS) * HEAD_DIM], axis=-1)
    q = rmsnorm(q.reshape(b, s, N_HEADS, HEAD_DIM), q_gain)
    k = rmsnorm(k.reshape(b, s, N_KV_HEADS, HEAD_DIM), k_gain)
    v = v.reshape(b, s, N_KV_HEADS, HEAD_DIM)
    row, col = grid_positions(s)
    q = axial_rope(q, row, col)
    k = axial_rope(k, row, col)
    nblk = s // Q_BLOCK
    qb = q.reshape(b, nblk, Q_BLOCK, N_KV_HEADS, KV_GROUP, HEAD_DIM).transpose(1, 0, 2, 3, 4, 5)
    scale = HEAD_DIM ** -0.5

    def block(qi):
        sc = jnp.einsum('bqhgd,bkhd->bhgqk', qi, k).astype(jnp.float32) * scale
        p = jax.nn.softmax(sc, axis=-1).astype(v.dtype)
        return jnp.einsum('bhgqk,bkhd->bqhgd', p, v)

    o = lax.map(block, qb)
    o = o.transpose(1, 0, 2, 3, 4, 5).reshape(b, s, N_HEADS * HEAD_DIM)
    return o @ w_o.astype(h.dtype)


def _lru_combine(c1, c2):
    a1, b1 = c1
    a2, b2 = c2
    return a1 * a2, a2 * b1 + b2


def recurrent_mixer(h, w_in, conv_w, conv_b, w_a, b_a, w_i, b_i, lam, w_out):
    b, s, _ = h.shape
    xr, gate = jnp.split(h @ w_in.astype(h.dtype), 2, axis=-1)
    u = dwconv(xr, conv_w, conv_b, REC_CONV_LEFT)
    uf = u.astype(jnp.float32)
    ub = uf.reshape(b, s, N_RNN_BLOCKS, RNN_BLOCK)
    r = jax.nn.sigmoid(jnp.einsum('bsnk,enkj->ebsnj', ub, w_a.astype(jnp.float32)).reshape(2, b, s, D_RNN)
                       + b_a.astype(jnp.float32)[:, None, None, :])
    ig = jax.nn.sigmoid(jnp.einsum('bsnk,enkj->ebsnj', ub, w_i.astype(jnp.float32)).reshape(2, b, s, D_RNN)
                        + b_i.astype(jnp.float32)[:, None, None, :])
    log_a = -LRU_C * r * jax.nn.softplus(-lam.astype(jnp.float32))[:, None, None, :]
    a = jnp.exp(log_a)
    bt = jnp.sqrt(-jnp.expm1(2.0 * log_a)) * (ig * uf[None])
    _, h_f = lax.associative_scan(_lru_combine, (a[0], bt[0]), axis=1)
    _, h_b = lax.associative_scan(_lru_combine, (a[1], bt[1]), axis=1, reverse=True)
    y = (h_f + h_b).astype(h.dtype) * jax.nn.gelu(gate)
    return y @ w_out.astype(h.dtype)


def conv_ffn(h, w_in, conv_w, conv_b, w_down):
    g, v = jnp.split(h @ w_in.astype(h.dtype), 2, axis=-1)
    g = dwconv(g, conv_w, conv_b, (FFN_CONV_W - 1) // 2)
    return (jax.nn.gelu(g) * v) @ w_down.astype(h.dtype)


def _trunk(x, norm_mix, norm_ffn, norm_final, attn_w_qkv, attn_q_gain, attn_k_gain, attn_w_o,
           rec_w_in, rec_conv_w, rec_conv_b, rec_w_a, rec_b_a, rec_w_i, rec_b_i, rec_lambda, rec_w_out,
           ffn_w_in, ffn_conv_w, ffn_conv_b, ffn_w_down):
    for i in range(DEPTH):
        h = rmsnorm(x, norm_mix[i])
        j = i // 2
        if i % 2 == 0:
            m = attention_mixer(h, attn_w_qkv[j], attn_q_gain[j], attn_k_gain[j], attn_w_o[j])
        else:
            m = recurrent_mixer(h, rec_w_in[j], rec_conv_w[j], rec_conv_b[j], rec_w_a[j], rec_b_a[j],
                                rec_w_i[j], rec_b_i[j], rec_lambda[j], rec_w_out[j])
        x = x + m
        x = x + conv_ffn(rmsnorm(x, norm_ffn[i]), ffn_w_in[i], ffn_conv_w[i], ffn_conv_b[i], ffn_w_down[i])
    return rmsnorm(x, norm_final)


def setup_inputs(seed: int = 0) -> dict:
    key = jax.random.key(seed)
    ks = jax.random.split(key, 24)
    f32 = jnp.float32

    def nrm(k, shape, fan_in, scale=1.0):
        return jax.random.normal(k, shape, f32) * (scale * fan_in ** -0.5)

    def gain(k, shape):
        return 1.0 + 0.05 * jax.random.normal(k, shape, f32)

    def bias(k, shape):
        return 0.02 * jax.random.normal(k, shape, f32)

    a_c = jax.random.uniform(ks[16], (N_REC, 2, D_RNN), f32, 0.9, 0.999)
    sig = a_c ** (1.0 / LRU_C)
    rec_lambda = jnp.log(sig) - jnp.log1p(-sig)
    return {
        'x_prompt': jax.random.normal(ks[0], (BATCH, SEQ, D_MODEL), f32),
        'x_sample': jax.random.normal(ks[1], (DEC_BATCH, DEC_SEQ, D_MODEL), f32),
        'norm_mix': gain(ks[2], (DEPTH, D_MODEL)),
        'norm_ffn': gain(ks[3], (DEPTH, D_MODEL)),
        'norm_final': gain(ks[4], (D_MODEL,)),
        'attn_w_qkv': nrm(ks[5], (N_ATTN, D_MODEL, QKV_DIM), D_MODEL),
        'attn_q_gain': gain(ks[6], (N_ATTN, HEAD_DIM)),
        'attn_k_gain': gain(ks[7], (N_ATTN, HEAD_DIM)),
        'attn_w_o': nrm(ks[8], (N_ATTN, N_HEADS * HEAD_DIM, D_MODEL), N_HEADS * HEAD_DIM, 0.5),
        'rec_w_in': nrm(ks[9], (N_REC, D_MODEL, 2 * D_RNN), D_MODEL),
        'rec_conv_w': nrm(ks[10], (N_REC, REC_CONV_W, D_RNN), REC_CONV_W),
        'rec_conv_b': bias(ks[11], (N_REC, D_RNN)),
        'rec_w_a': nrm(ks[12], (N_REC, 2, N_RNN_BLOCKS, RNN_BLOCK, RNN_BLOCK), RNN_BLOCK),
        'rec_b_a': bias(ks[13], (N_REC, 2, D_RNN)),
        'rec_w_i': nrm(ks[14], (N_REC, 2, N_RNN_BLOCKS, RNN_BLOCK, RNN_BLOCK), RNN_BLOCK),
        'rec_b_i': bias(ks[15], (N_REC, 2, D_RNN)),
        'rec_lambda': rec_lambda,
        'rec_w_out': nrm(ks[17], (N_REC, D_RNN, D_MODEL), D_RNN, 0.5),
        'ffn_w_in': nrm(ks[18], (DEPTH, D_MODEL, 2 * D_FF), D_MODEL),
        'ffn_conv_w': nrm(ks[19], (DEPTH, FFN_CONV_W, D_FF), FFN_CONV_W),
        'ffn_conv_b': bias(ks[20], (DEPTH, D_FF)),
        'ffn_w_down': nrm(ks[21], (DEPTH, D_FF, D_MODEL), D_FF, 0.5),
    }


def reference(x_prompt, x_sample, norm_mix, norm_ffn, norm_final, attn_w_qkv, attn_q_gain, attn_k_gain,
              attn_w_o, rec_w_in, rec_conv_w, rec_conv_b, rec_w_a, rec_b_a, rec_w_i, rec_b_i, rec_lambda,
              rec_w_out, ffn_w_in, ffn_conv_w, ffn_conv_b, ffn_w_down):
    y_prompt = _trunk(x_prompt, norm_mix, norm_ffn, norm_final, attn_w_qkv, attn_q_gain, attn_k_gain, attn_w_o,
                      rec_w_in, rec_conv_w, rec_conv_b, rec_w_a, rec_b_a, rec_w_i, rec_b_i, rec_lambda, rec_w_out,
                      ffn_w_in, ffn_conv_w, ffn_conv_b, ffn_w_down)
    y_sample = _trunk(x_sample, norm_mix, norm_ffn, norm_final, attn_w_qkv, attn_q_gain, attn_k_gain, attn_w_o,
                      rec_w_in, rec_conv_w, rec_conv_b, rec_w_a, rec_b_a, rec_w_i, rec_b_i, rec_lambda, rec_w_out,
                      ffn_w_in, ffn_conv_w, ffn_conv_b, ffn_w_down)
    return (y_prompt, y_sample)
```

```python
import functools

import jax
import jax.numpy as jnp
from jax import lax
from jax.experimental import pallas as pl
from jax.experimental.pallas import tpu as pltpu

D_MODEL = 2048
N_HEADS = 16
N_KV_HEADS = 8
HEAD_DIM = 128
KV_GROUP = N_HEADS // N_KV_HEADS
QKV_DIM = (N_HEADS + 2 * N_KV_HEADS) * HEAD_DIM
ROPE_THETA = 10000.0
GRID_W = 64
AXIS_DIM = HEAD_DIM // 2
ROT_HALF = AXIS_DIM // 2
D_RNN = D_MODEL
N_RNN_BLOCKS = 16
RNN_BLOCK = D_RNN // N_RNN_BLOCKS
REC_CONV_W = 4
LRU_C = 8.0
D_FF = 5632
NORM_EPS = 1e-6

BF16_SUBLANES = 16
HALO = BF16_SUBLANES
MIB = 1024 * 1024

F32 = jnp.float32
BF16 = jnp.bfloat16


def _vmem_limit(estimate_bytes):
    return int(min(estimate_bytes * 1.25 + 8 * MIB, 60 * MIB))


def _rms(x, gain):
    ms = jnp.mean(x * x, axis=-1, keepdims=True)
    return x * lax.rsqrt(ms + NORM_EPS) * gain


def _norm_matmul_kernel(x_ref, g_ref, w_ref, *rest, rope, tn):
    if rope:
        qg_ref, kg_ref, cos_ref, sin_ref, o_ref, xn_ref = rest
    else:
        o_ref, xn_ref = rest
    j = pl.program_id(1)

    @pl.when(j == 0)
    def _():
        xn_ref[...] = _rms(x_ref[...], g_ref[...]).astype(BF16)

    y = jnp.dot(xn_ref[...], w_ref[...], preferred_element_type=F32)

    if not rope:
        o_ref[...] = y.astype(o_ref.dtype)
        return

    q_tiles = N_HEADS * HEAD_DIM // tn
    k_tiles = N_KV_HEADS * HEAD_DIM // tn

    def head_norm_rope(gain, post_scale):
        cos = cos_ref[...]
        sin = sin_ref[...]
        lane = lax.broadcasted_iota(jnp.int32, (1, HEAD_DIM), 1)
        low = (lane % AXIS_DIM) < ROT_HALF
        for h in range(tn // HEAD_DIM):
            yh = _rms(y[:, h * HEAD_DIM:(h + 1) * HEAD_DIM], gain)
            partner = jnp.where(low, pltpu.roll(yh, HEAD_DIM - ROT_HALF, 1),
                                pltpu.roll(yh, ROT_HALF, 1))
            yh = yh * cos + partner * sin
            if post_scale is not None:
                yh = yh * post_scale
            o_ref[:, h * HEAD_DIM:(h + 1) * HEAD_DIM] = yh.astype(o_ref.dtype)

    @pl.when(j < q_tiles)
    def _():
        head_norm_rope(qg_ref[...], HEAD_DIM ** -0.5)

    @pl.when(jnp.logical_and(j >= q_tiles, j < q_tiles + k_tiles))
    def _():
        head_norm_rope(kg_ref[...], None)

    @pl.when(j >= q_tiles + k_tiles)
    def _():
        o_ref[...] = y.astype(o_ref.dtype)


def _norm_matmul(x2, gain, w, *, seq, rope_args=None, tm=512, tn=512):
    t, d = x2.shape
    n = w.shape[1]
    rope = rope_args is not None
    in_specs = [
        pl.BlockSpec((tm, d), lambda i, j: (i, 0)),
        pl.BlockSpec((1, d), lambda i, j: (0, 0)),
        pl.BlockSpec((d, tn), lambda i, j: (0, j)),
    ]
    args = [x2, gain.reshape(1, d), w]
    if rope:
        q_gain, k_gain, cos, sin = rope_args
        tiles_per_seq = seq // tm
        in_specs += [
            pl.BlockSpec((1, HEAD_DIM), lambda i, j: (0, 0)),
            pl.BlockSpec((1, HEAD_DIM), lambda i, j: (0, 0)),
            pl.BlockSpec((tm, HEAD_DIM), lambda i, j: (i % tiles_per_seq, 0)),
            pl.BlockSpec((tm, HEAD_DIM), lambda i, j: (i % tiles_per_seq, 0)),
        ]
        args += [q_gain.reshape(1, HEAD_DIM), k_gain.reshape(1, HEAD_DIM), cos, sin]
    est = (2 * tm * d * 4 + tm * d * 2 + 2 * d * tn * 2 + 2 * tm * tn * 2
           + 4 * tm * HEAD_DIM * 4 + 2 * tm * tn * 4)
    return pl.pallas_call(
        functools.partial(_norm_matmul_kernel, rope=rope, tn=tn),
        grid=(t // tm, n // tn),
        in_specs=in_specs,
        out_specs=pl.BlockSpec((tm, tn), lambda i, j: (i, j)),
        out_shape=jax.ShapeDtypeStruct((t, n), BF16),
        scratch_shapes=[pltpu.VMEM((tm, d), BF16)],
        compiler_params=pltpu.CompilerParams(
            dimension_semantics=("parallel", "arbitrary"),
            vmem_limit_bytes=_vmem_limit(est)),
        name="norm_matmul_rope" if rope else "norm_matmul",
    )(*args)


def _rope_tables(seq):
    pos = jnp.arange(seq, dtype=jnp.int32)
    row = (pos // GRID_W).astype(F32)
    col = (pos % GRID_W).astype(F32)
    inv = ROPE_THETA ** (-jnp.arange(0, AXIS_DIM, 2, dtype=F32) / AXIS_DIM)
    ang_r = row[:, None] * inv[None, :]
    ang_c = col[:, None] * inv[None, :]
    cos = jnp.concatenate([jnp.cos(ang_r)] * 2 + [jnp.cos(ang_c)] * 2, axis=-1)
    sin = jnp.concatenate([-jnp.sin(ang_r), jnp.sin(ang_r),
                           -jnp.sin(ang_c), jnp.sin(ang_c)], axis=-1)
    return cos, sin


def _attn_kernel(q_ref, k_ref, v_ref, o_ref, q2_ref, m_ref, l_ref, acc_ref, *, tk):
    tq = q_ref.shape[1]
    seq = k_ref.shape[1]
    q2_ref[:tq] = q_ref[0, :, :HEAD_DIM]
    q2_ref[tq:] = q_ref[0, :, HEAD_DIM:]
    m_ref[...] = jnp.full(m_ref.shape, -jnp.inf, F32)
    l_ref[...] = jnp.zeros(l_ref.shape, F32)
    acc_ref[...] = jnp.zeros(acc_ref.shape, F32)

    def body(c, carry):
        r = pl.multiple_of(c * tk, tk)
        k = k_ref[0, pl.ds(r, tk), :]
        v = v_ref[0, pl.ds(r, tk), :]
        s = lax.dot_general(q2_ref[...], k, (((1,), (1,)), ((), ())),
                            preferred_element_type=F32)
        m_prev = m_ref[...]
        m_new = jnp.maximum(m_prev, jnp.max(s, axis=-1, keepdims=True))
        alpha = jnp.exp(m_prev - m_new)
        p = jnp.exp(s - m_new)
        l_ref[...] = alpha * l_ref[...] + jnp.sum(p, axis=-1, keepdims=True)
        acc_ref[...] = alpha * acc_ref[...] + jnp.dot(
            p.astype(BF16), v, preferred_element_type=F32)
        m_ref[...] = m_new
        return carry

    lax.fori_loop(0, seq // tk, body, 0)
    o = acc_ref[...] / l_ref[...]
    o_ref[0, :, :HEAD_DIM] = o[:tq].astype(o_ref.dtype)
    o_ref[0, :, HEAD_DIM:] = o[tq:].astype(o_ref.dtype)


def _attention(qkv, *, tq=256, tk=512):
    b, s, _ = qkv.shape
    gw = KV_GROUP * HEAD_DIM
    k_col0 = N_HEADS
    v_col0 = N_HEADS + N_KV_HEADS
    est = (2 * tq * gw * 2 * 2 + 2 * 2 * s * HEAD_DIM * 2 + KV_GROUP * tq * HEAD_DIM * 6
           + 4 * KV_GROUP * tq * tk * 4)
    return pl.pallas_call(
        functools.partial(_attn_kernel, tk=tk),
        grid=(b, N_KV_HEADS, s // tq),
        in_specs=[
            pl.BlockSpec((1, tq, gw), lambda bi, h, i: (bi, i, h)),
            pl.BlockSpec((1, s, HEAD_DIM), lambda bi, h, i: (bi, 0, k_col0 + h)),
            pl.BlockSpec((1, s, HEAD_DIM), lambda bi, h, i: (bi, 0, v_col0 + h)),
        ],
        out_specs=pl.BlockSpec((1, tq, gw), lambda bi, h, i: (bi, i, h)),
        out_shape=jax.ShapeDtypeStruct((b, s, N_HEADS * HEAD_DIM), BF16),
        scratch_shapes=[
            pltpu.VMEM((KV_GROUP * tq, HEAD_DIM), BF16),
            pltpu.VMEM((KV_GROUP * tq, 1), F32),
            pltpu.VMEM((KV_GROUP * tq, 1), F32),
            pltpu.VMEM((KV_GROUP * tq, HEAD_DIM), F32),
        ],
        compiler_params=pltpu.CompilerParams(
            dimension_semantics=("parallel", "parallel", "arbitrary"),
            vmem_limit_bytes=_vmem_limit(est)),
        name="flash_attention",
    )(qkv, qkv, qkv)


def _matmul_residual_kernel(a_ref, w_ref, x_ref, o_ref):
    o_ref[...] = x_ref[...] + jnp.dot(a_ref[...], w_ref[...], preferred_element_type=F32)


def _matmul_residual(a, w, x2, *, tm=512):
    t, k = a.shape
    d = w.shape[1]
    est = 2 * tm * k * 2 + 2 * k * d * 2 + 4 * tm * d * 4
    return pl.pallas_call(
        _matmul_residual_kernel,
        grid=(t // tm,),
        in_specs=[
            pl.BlockSpec((tm, k), lambda i: (i, 0)),
            pl.BlockSpec((k, d), lambda i: (0, 0)),
            pl.BlockSpec((tm, d), lambda i: (i, 0)),
        ],
        out_specs=pl.BlockSpec((tm, d), lambda i: (i, 0)),
        out_shape=jax.ShapeDtypeStruct((t, d), F32),
        compiler_params=pltpu.CompilerParams(
            dimension_semantics=("parallel",),
            vmem_limit_bytes=_vmem_limit(est)),
        name="matmul_residual",
    )(a, w, x2)


def _ffn_kernel(x_ref, xp_ref, xq_ref, g_ref, wg_ref, wv_ref, cw_ref, cb_ref, wd_ref,
                *rest, tiles_per_seq, final):
    if final:
        fg_ref, o_ref, xn_ref, gx_ref, acc_ref = rest
    else:
        o_ref, xn_ref, gx_ref, acc_ref = rest
    tm = x_ref.shape[0]
    i = pl.program_id(0)
    j = pl.program_id(1)

    @pl.when(j == 0)
    def _():
        gain = g_ref[...]
        first = (i % tiles_per_seq) == 0
        last = (i % tiles_per_seq) == tiles_per_seq - 1
        xp = jnp.where(first, 0.0, _rms(xp_ref[...], gain))
        xq = jnp.where(last, 0.0, _rms(xq_ref[...], gain))
        xn_ref[:HALO] = xp.astype(BF16)
        xn_ref[HALO:HALO + tm] = _rms(x_ref[...], gain).astype(BF16)
        xn_ref[HALO + tm:] = xq.astype(BF16)
        acc_ref[...] = jnp.zeros(acc_ref.shape, F32)

    gx_ref[...] = jnp.dot(xn_ref[...], wg_ref[...], preferred_element_type=F32)
    v = jnp.dot(xn_ref[HALO:HALO + tm], wv_ref[...], preferred_element_type=F32)
    cw = cw_ref[...]
    g = (cw[0:1] * gx_ref[pl.ds(HALO - 1, tm), :]
         + cw[1:2] * gx_ref[pl.ds(HALO, tm), :]
         + cw[2:3] * gx_ref[pl.ds(HALO + 1, tm), :]
         + cb_ref[...])
    act = (jax.nn.gelu(g) * v).astype(BF16)
    acc_ref[...] += jnp.dot(act, wd_ref[...], preferred_element_type=F32)

    @pl.when(j == pl.num_programs(1) - 1)
    def _():
        y = x_ref[...] + acc_ref[...]
        if final:
            y = _rms(y, fg_ref[...])
        o_ref[...] = y


def _ffn(x2, gain, w_in, conv_w, conv_b, w_down, *, seq, final_gain=None, tm=512, tn=512):
    t, d = x2.shape
    f = w_down.shape[0]
    nj = f // tn
    hb = tm // HALO
    n_hblocks = t // HALO
    final = final_gain is not None
    in_specs = [
        pl.BlockSpec((tm, d), lambda i, j: (i, 0)),
        pl.BlockSpec((HALO, d), lambda i, j: (jnp.maximum(i * hb - 1, 0), 0)),
        pl.BlockSpec((HALO, d), lambda i, j: (jnp.minimum((i + 1) * hb, n_hblocks - 1), 0)),
        pl.BlockSpec((1, d), lambda i, j: (0, 0)),
        pl.BlockSpec((d, tn), lambda i, j: (0, j)),
        pl.BlockSpec((d, tn), lambda i, j: (0, j + nj)),
        pl.BlockSpec((3, tn), lambda i, j: (0, j)),
        pl.BlockSpec((1, tn), lambda i, j: (0, j)),
        pl.BlockSpec((tn, d), lambda i, j: (j, 0)),
    ]
    args = [x2, x2, x2, gain.reshape(1, d), w_in, w_in, conv_w, conv_b.reshape(1, f), w_down]
    if final:
        in_specs.append(pl.BlockSpec((1, d), lambda i, j: (0, 0)))
        args.append(final_gain.reshape(1, d))
    est = (2 * tm * d * 4 + 4 * HALO * d * 4 + (tm + 2 * HALO) * d * 2 + 3 * 2 * d * tn * 2
           + (tm + 2 * HALO) * tn * 4 + tm * d * 4 + 2 * tm * d * 4 + 4 * tm * tn * 4)
    return pl.pallas_call(
        functools.partial(_ffn_kernel, tiles_per_seq=seq // tm, final=final),
        grid=(t // tm, nj),
        in_specs=in_specs,
        out_specs=pl.BlockSpec((tm, d), lambda i, j: (i, 0)),
        out_shape=jax.ShapeDtypeStruct((t, d), F32),
        scratch_shapes=[
            pltpu.VMEM((tm + 2 * HALO, d), BF16),
            pltpu.VMEM((tm + 2 * HALO, tn), F32),
            pltpu.VMEM((tm, d), F32),
        ],
        compiler_params=pltpu.CompilerParams(
            dimension_semantics=("parallel", "arbitrary"),
            vmem_limit_bytes=_vmem_limit(est)),
        name="conv_ffn_final" if final else "conv_ffn",
    )(*args)


def _sigmoid(x):
    return 0.5 * jnp.tanh(0.5 * x) + 0.5


def _softplus(x):
    return jnp.maximum(x, 0.0) + jnp.log1p(jnp.exp(-jnp.abs(x)))


def _group_scan(a, b, reverse):
    sub = lax.broadcasted_iota(jnp.int32, (1, 8, a.shape[-1]), 1)
    for dist in (1, 2, 4):
        if reverse:
            keep = sub < 8 - dist
            shift = 8 - dist
        else:
            keep = sub >= dist
            shift = dist
        a_s = jnp.where(keep, pltpu.roll(a, shift, 1), 1.0)
        b_s = jnp.where(keep, pltpu.roll(b, shift, 1), 0.0)
        b = a * b_s + b
        a = a * a_s
    return a, b


def _chunk_scan(a, b, h, reverse):
    tc, c = a.shape
    groups = tc // 8
    a3, b3 = _group_scan(a.reshape(groups, 8, c), b.reshape(groups, 8, c), reverse)
    outs = [None] * groups
    order = range(groups - 1, -1, -1) if reverse else range(groups)
    for g in order:
        hg = b3[g] + a3[g] * h
        outs[g] = hg
        h = hg[0:1] if reverse else hg[7:8]
    return jnp.concatenate(outs, axis=0), h


def _rec_kernel(xr_ref, gt_ref, cw_ref, cb_ref, wg_ref, ba_ref, bi_ref, lam_ref,
                y_ref, u_ref, hf_ref, *, tc):
    seq = xr_ref.shape[1]
    nc = seq // tc
    c = xr_ref.shape[2]
    cw = cw_ref[...]
    cb = cb_ref[...]
    sp = _softplus(-lam_ref[...])
    win_rows = tc + 2 * HALO

    def conv(ci):
        r0 = pl.multiple_of(ci * tc, tc)
        rp = pl.multiple_of(jnp.maximum(r0 - HALO, 0), HALO)
        rn = pl.multiple_of(jnp.minimum(r0 + tc, seq - HALO), HALO)
        prev = jnp.where(ci == 0, 0.0, xr_ref[0, pl.ds(rp, HALO), :].astype(F32))
        nxt = jnp.where(ci == nc - 1, 0.0, xr_ref[0, pl.ds(rn, HALO), :].astype(F32))
        main = xr_ref[0, pl.ds(r0, tc), :].astype(F32)
        win = jnp.concatenate([prev, main, nxt], axis=0)

        def shifted(k):
            if k == 0:
                return main
            return pltpu.roll(win, (-k) % win_rows, 0)[HALO:HALO + tc]

        return (cw[0:1] * shifted(-2) + cw[1:2] * shifted(-1) + cw[2:3] * main
                + cw[3:4] * shifted(1) + cb)

    def gates(u, e):
        z = jnp.dot(u.astype(BF16), wg_ref[e, 0], preferred_element_type=F32)
        r = _sigmoid(z[:, :c] + ba_ref[e:e + 1, :])
        ig = _sigmoid(z[:, c:] + bi_ref[e:e + 1, :])
        log_a = (-LRU_C) * r * sp[e:e + 1]
        a = jnp.exp(log_a)
        bt = jnp.sqrt(-jnp.tanh(log_a) * (a * a + 1.0)) * (ig * u)
        return a, bt

    def fwd(ci, h):
        r0 = pl.multiple_of(ci * tc, tc)
        u = conv(ci)
        u_ref[pl.ds(r0, tc), :] = u
        a, bt = gates(u, 0)
        hs, h = _chunk_scan(a, bt, h, reverse=False)
        hf_ref[pl.ds(r0, tc), :] = hs
        return h

    def bwd(k, h):
        ci = nc - 1 - k
        r0 = pl.multiple_of(ci * tc, tc)
        u = u_ref[pl.ds(r0, tc), :]
        a, bt = gates(u, 1)
        hs, h = _chunk_scan(a, bt, h, reverse=True)
        gate = gt_ref[0, pl.ds(r0, tc), :].astype(F32)
        y = (hf_ref[pl.ds(r0, tc), :] + hs) * jax.nn.gelu(gate)
        y_ref[0, pl.ds(r0, tc), :] = y.astype(y_ref.dtype)
        return h

    h0 = jnp.zeros((1, c), F32)
    lax.fori_loop(0, nc, fwd, h0)
    lax.fori_loop(0, nc, bwd, h0)


def _rglru(xg, conv_w, conv_b, w_gates, b_a, b_i, lam, *, tc=256):
    b, s, _ = xg.shape
    c = RNN_BLOCK
    est = 2 * 2 * s * c * 2 + 2 * s * c * 2 + 2 * s * c * 4 + 2 * 2 * c * 2 * c * 2 + 64 * tc * c * 4
    return pl.pallas_call(
        functools.partial(_rec_kernel, tc=tc),
        grid=(b, N_RNN_BLOCKS),
        in_specs=[
            pl.BlockSpec((1, s, c), lambda bi, n: (bi, 0, n)),
            pl.BlockSpec((1, s, c), lambda bi, n: (bi, 0, N_RNN_BLOCKS + n)),
            pl.BlockSpec((REC_CONV_W, c), lambda bi, n: (0, n)),
            pl.BlockSpec((1, c), lambda bi, n: (0, n)),
            pl.BlockSpec((2, 1, c, 2 * c), lambda bi, n: (0, n, 0, 0)),
            pl.BlockSpec((2, c), lambda bi, n: (0, n)),
            pl.BlockSpec((2, c), lambda bi, n: (0, n)),
            pl.BlockSpec((2, c), lambda bi, n: (0, n)),
        ],
        out_specs=pl.BlockSpec((1, s, c), lambda bi, n: (bi, 0, n)),
        out_shape=jax.ShapeDtypeStruct((b, s, D_RNN), BF16),
        scratch_shapes=[pltpu.VMEM((s, c), F32), pltpu.VMEM((s, c), F32)],
        compiler_params=pltpu.CompilerParams(
            dimension_semantics=("parallel", "parallel"),
            vmem_limit_bytes=_vmem_limit(est)),
        name="rglru",
    )(xg, xg, conv_w, conv_b.reshape(1, D_RNN), w_gates, b_a, b_i, lam)


def _trunk(x, p):
    b, s, d = x.shape
    t = b * s
    x2 = x.reshape(t, d)
    cos, sin = _rope_tables(s)

    qkv = _norm_matmul(x2, p["norm_mix"][0], p["w_qkv"], seq=s,
                       rope_args=(p["q_gain"], p["k_gain"], cos, sin))
    attn = _attention(qkv.reshape(b, s, QKV_DIM))
    x2 = _matmul_residual(attn.reshape(t, N_HEADS * HEAD_DIM), p["w_o"], x2)
    x2 = _ffn(x2, p["norm_ffn"][0], p["ffn_w_in"][0], p["ffn_conv_w"][0], p["ffn_conv_b"][0],
              p["ffn_w_down"][0], seq=s)

    xg = _norm_matmul(x2, p["norm_mix"][1], p["rec_w_in"], seq=s)
    y = _rglru(xg.reshape(b, s, 2 * D_RNN), p["rec_conv_w"], p["rec_conv_b"], p["rec_w_gates"],
               p["rec_b_a"], p["rec_b_i"], p["rec_lambda"])
    x2 = _matmul_residual(y.reshape(t, D_RNN), p["rec_w_out"], x2)
    x2 = _ffn(x2, p["norm_ffn"][1], p["ffn_w_in"][1], p["ffn_conv_w"][1], p["ffn_conv_b"][1],
              p["ffn_w_down"][1], seq=s, final_gain=p["norm_final"])
    return x2.reshape(b, s, d)


def kernel(x_prompt, x_sample, norm_mix, norm_ffn, norm_final, attn_w_qkv, attn_q_gain, attn_k_gain, attn_w_o, rec_w_in, rec_conv_w, rec_conv_b, rec_w_a, rec_b_a, rec_w_i, rec_b_i, rec_lambda, rec_w_out, ffn_w_in, ffn_conv_w, ffn_conv_b, ffn_w_down):
    assert norm_mix.shape[0] == 2 and attn_w_qkv.shape[0] == 1 and rec_w_in.shape[0] == 1
    p = {
        "norm_mix": norm_mix, "norm_ffn": norm_ffn, "norm_final": norm_final,
        "w_qkv": attn_w_qkv[0].astype(BF16),
        "q_gain": attn_q_gain[0], "k_gain": attn_k_gain[0],
        "w_o": attn_w_o[0].astype(BF16),
        "rec_w_in": rec_w_in[0].astype(BF16),
        "rec_conv_w": rec_conv_w[0], "rec_conv_b": rec_conv_b[0],
        "rec_w_gates": jnp.concatenate([rec_w_a[0], rec_w_i[0]], axis=-1).astype(BF16),
        "rec_b_a": rec_b_a[0], "rec_b_i": rec_b_i[0], "rec_lambda": rec_lambda[0],
        "rec_w_out": rec_w_out[0].astype(BF16),
        "ffn_w_in": ffn_w_in.astype(BF16), "ffn_conv_w": ffn_conv_w, "ffn_conv_b": ffn_conv_b,
        "ffn_w_down": ffn_w_down.astype(BF16),
    }
    return (_trunk(x_prompt, p), _trunk(x_sample, p))
```

```python
import functools

import jax
import jax.numpy as jnp
from jax import lax
from jax.experimental import pallas as pl
from jax.experimental.pallas import tpu as pltpu

D_MODEL = 2048
N_HEADS = 16
N_KV_HEADS = 8
HEAD_DIM = 128
KV_GROUP = N_HEADS // N_KV_HEADS
QKV_DIM = (N_HEADS + 2 * N_KV_HEADS) * HEAD_DIM
ROPE_THETA = 10000.0
GRID_W = 64
AXIS_DIM = HEAD_DIM // 2
ROT_HALF = AXIS_DIM // 2
D_RNN = D_MODEL
N_RNN_BLOCKS = 16
RNN_BLOCK = D_RNN // N_RNN_BLOCKS
REC_CONV_W = 4
LRU_C = 8.0
D_FF = 5632
NORM_EPS = 1e-6
LOG2_E = 1.4426950408889634

BF16_SUBLANES = 16
HALO = BF16_SUBLANES
MIB = 1024 * 1024

F32 = jnp.float32
BF16 = jnp.bfloat16


def _vmem_limit(estimate_bytes):
    return int(min(estimate_bytes * 1.25 + 8 * MIB, 60 * MIB))


def _rms(x, gain):
    ms = jnp.mean(x * x, axis=-1, keepdims=True)
    return x * lax.rsqrt(ms + NORM_EPS) * gain


def _norm_matmul_kernel(x_ref, g_ref, w_ref, *rest, rope, tn):
    if rope:
        qg_ref, kg_ref, cos_ref, sin_ref, o_ref, xn_ref = rest
    else:
        o_ref, xn_ref = rest
    j = pl.program_id(1)

    @pl.when(j == 0)
    def _():
        xn_ref[...] = _rms(x_ref[...], g_ref[...]).astype(BF16)

    y = jnp.dot(xn_ref[...], w_ref[...], preferred_element_type=F32)

    if not rope:
        o_ref[...] = y.astype(o_ref.dtype)
        return

    q_tiles = N_HEADS * HEAD_DIM // tn
    k_tiles = N_KV_HEADS * HEAD_DIM // tn

    def head_norm_rope(gain, post_scale):
        cos = cos_ref[...]
        sin = sin_ref[...]
        lane = lax.broadcasted_iota(jnp.int32, (1, HEAD_DIM), 1)
        low = (lane % AXIS_DIM) < ROT_HALF
        for h in range(tn // HEAD_DIM):
            yh = _rms(y[:, h * HEAD_DIM:(h + 1) * HEAD_DIM], gain)
            partner = jnp.where(low, pltpu.roll(yh, HEAD_DIM - ROT_HALF, 1),
                                pltpu.roll(yh, ROT_HALF, 1))
            yh = yh * cos + partner * sin
            if post_scale is not None:
                yh = yh * post_scale
            o_ref[:, h * HEAD_DIM:(h + 1) * HEAD_DIM] = yh.astype(o_ref.dtype)

    @pl.when(j < q_tiles)
    def _():
        head_norm_rope(qg_ref[...], HEAD_DIM ** -0.5 * LOG2_E)

    @pl.when(jnp.logical_and(j >= q_tiles, j < q_tiles + k_tiles))
    def _():
        head_norm_rope(kg_ref[...], None)

    @pl.when(j >= q_tiles + k_tiles)
    def _():
        o_ref[...] = y.astype(o_ref.dtype)


def _norm_matmul(x2, gain, w, *, seq, rope_args=None, tm=512, tn=512):
    t, d = x2.shape
    n = w.shape[1]
    rope = rope_args is not None
    in_specs = [
        pl.BlockSpec((tm, d), lambda i, j: (i, 0)),
        pl.BlockSpec((1, d), lambda i, j: (0, 0)),
        pl.BlockSpec((d, tn), lambda i, j: (0, j)),
    ]
    args = [x2, gain.reshape(1, d), w]
    if rope:
        q_gain, k_gain, cos, sin = rope_args
        tiles_per_seq = seq // tm
        in_specs += [
            pl.BlockSpec((1, HEAD_DIM), lambda i, j: (0, 0)),
            pl.BlockSpec((1, HEAD_DIM), lambda i, j: (0, 0)),
            pl.BlockSpec((tm, HEAD_DIM), lambda i, j: (i % tiles_per_seq, 0)),
            pl.BlockSpec((tm, HEAD_DIM), lambda i, j: (i % tiles_per_seq, 0)),
        ]
        args += [q_gain.reshape(1, HEAD_DIM), k_gain.reshape(1, HEAD_DIM), cos, sin]
    est = (2 * tm * d * 4 + tm * d * 2 + 2 * d * tn * 2 + 2 * tm * tn * 2
           + 4 * tm * HEAD_DIM * 4 + 2 * tm * tn * 4)
    return pl.pallas_call(
        functools.partial(_norm_matmul_kernel, rope=rope, tn=tn),
        grid=(t // tm, n // tn),
        in_specs=in_specs,
        out_specs=pl.BlockSpec((tm, tn), lambda i, j: (i, j)),
        out_shape=jax.ShapeDtypeStruct((t, n), BF16),
        scratch_shapes=[pltpu.VMEM((tm, d), BF16)],
        compiler_params=pltpu.CompilerParams(
            dimension_semantics=("parallel", "arbitrary"),
            vmem_limit_bytes=_vmem_limit(est)),
        name="norm_matmul_rope" if rope else "norm_matmul",
    )(*args)


def _rope_tables(seq):
    pos = jnp.arange(seq, dtype=jnp.int32)
    row = (pos // GRID_W).astype(F32)
    col = (pos % GRID_W).astype(F32)
    inv = ROPE_THETA ** (-jnp.arange(0, AXIS_DIM, 2, dtype=F32) / AXIS_DIM)
    ang_r = row[:, None] * inv[None, :]
    ang_c = col[:, None] * inv[None, :]
    cos = jnp.concatenate([jnp.cos(ang_r)] * 2 + [jnp.cos(ang_c)] * 2, axis=-1)
    sin = jnp.concatenate([-jnp.sin(ang_r), jnp.sin(ang_r),
                           -jnp.sin(ang_c), jnp.sin(ang_c)], axis=-1)
    return cos, sin


def _attn_kernel(q_ref, k_ref, v_ref, o_ref, q2_ref, m_ref, l_ref, acc_ref,
                 s_a, s_b, p_a, p_b, al_a, al_b, *, tk, rb):
    tq = q_ref.shape[1]
    rows2 = KV_GROUP * tq
    seq = k_ref.shape[1]
    n = seq // tk
    lanes = HEAD_DIM
    q2_ref[:tq] = q_ref[0, :, :HEAD_DIM]
    q2_ref[tq:] = q_ref[0, :, HEAD_DIM:]
    m_ref[...] = jnp.full(m_ref.shape, -jnp.inf, F32)
    l_ref[...] = jnp.zeros(l_ref.shape, F32)
    acc_ref[...] = jnp.zeros(acc_ref.shape, F32)

    def scores(j, s_ref):
        r = pl.multiple_of(j * tk, tk)
        s_ref[...] = lax.dot_general(q2_ref[...], k_ref[0, pl.ds(r, tk), :],
                                     (((1,), (1,)), ((), ())), preferred_element_type=F32)

    def softmax(s_ref, p_ref, al_ref):
        for b0 in range(0, rows2, rb):
            rows = slice(b0, b0 + rb)
            tiles = [s_ref[rows, t * lanes:(t + 1) * lanes] for t in range(tk // lanes)]
            mx = tiles[0]
            for tile in tiles[1:]:
                mx = jnp.maximum(mx, tile)
            m_prev = m_ref[rows, :]
            m_new = jnp.maximum(m_prev, jnp.max(mx, axis=-1, keepdims=True))
            alpha = jnp.exp2(m_prev - m_new)
            psum = None
            for t, tile in enumerate(tiles):
                p = jnp.exp2(tile - m_new)
                p_ref[rows, t * lanes:(t + 1) * lanes] = p.astype(BF16)
                psum = p if psum is None else psum + p
            l_ref[rows, :] = alpha * l_ref[rows, :] + psum
            m_ref[rows, :] = m_new
            al_ref[rows, :] = alpha

    def weighted_values(j, p_ref, al_ref):
        r = pl.multiple_of(j * tk, tk)
        pv = jnp.dot(p_ref[...], v_ref[0, pl.ds(r, tk), :], preferred_element_type=F32)
        acc_ref[...] = al_ref[...] * acc_ref[...] + pv

    scores(0, s_a)
    p_b[...] = jnp.zeros(p_b.shape, BF16)
    al_b[...] = jnp.ones(al_b.shape, F32)

    def pair(c, carry):
        j = 2 * c
        scores(j + 1, s_b)
        softmax(s_a, p_a, al_a)
        weighted_values(jnp.maximum(j - 1, 0), p_b, al_b)
        scores(j + 2, s_a)
        softmax(s_b, p_b, al_b)
        weighted_values(j, p_a, al_a)
        return carry

    lax.fori_loop(0, n // 2 - 1, pair, 0)
    scores(n - 1, s_b)
    softmax(s_a, p_a, al_a)
    weighted_values(max(n - 3, 0), p_b, al_b)
    softmax(s_b, p_b, al_b)
    weighted_values(n - 2, p_a, al_a)
    weighted_values(n - 1, p_b, al_b)

    o = acc_ref[...] / jnp.sum(l_ref[...], axis=-1, keepdims=True)
    o_ref[0, :, :HEAD_DIM] = o[:tq].astype(o_ref.dtype)
    o_ref[0, :, HEAD_DIM:] = o[tq:].astype(o_ref.dtype)


def _attention(qkv, *, tq=256, tk=1024, rb=16):
    b, s, _ = qkv.shape
    assert s % (2 * tk) == 0
    gw = KV_GROUP * HEAD_DIM
    rows2 = KV_GROUP * tq
    k_col0 = N_HEADS
    v_col0 = N_HEADS + N_KV_HEADS
    est = (2 * tq * gw * 2 * 2 + 2 * 2 * s * HEAD_DIM * 2 + rows2 * HEAD_DIM * (2 + 5 * 4)
           + 2 * rows2 * tk * (4 + 2))
    return pl.pallas_call(
        functools.partial(_attn_kernel, tk=tk, rb=rb),
        grid=(b, N_KV_HEADS, s // tq),
        in_specs=[
            pl.BlockSpec((1, tq, gw), lambda bi, h, i: (bi, i, h)),
            pl.BlockSpec((1, s, HEAD_DIM), lambda bi, h, i: (bi, 0, k_col0 + h)),
            pl.BlockSpec((1, s, HEAD_DIM), lambda bi, h, i: (bi, 0, v_col0 + h)),
        ],
        out_specs=pl.BlockSpec((1, tq, gw), lambda bi, h, i: (bi, i, h)),
        out_shape=jax.ShapeDtypeStruct((b, s, N_HEADS * HEAD_DIM), BF16),
        scratch_shapes=[
            pltpu.VMEM((rows2, HEAD_DIM), BF16),
            pltpu.VMEM((rows2, HEAD_DIM), F32),
            pltpu.VMEM((rows2, HEAD_DIM), F32),
            pltpu.VMEM((rows2, HEAD_DIM), F32),
            pltpu.VMEM((rows2, tk), F32),
            pltpu.VMEM((rows2, tk), F32),
            pltpu.VMEM((rows2, tk), BF16),
            pltpu.VMEM((rows2, tk), BF16),
            pltpu.VMEM((rows2, HEAD_DIM), F32),
            pltpu.VMEM((rows2, HEAD_DIM), F32),
        ],
        compiler_params=pltpu.CompilerParams(
            dimension_semantics=("parallel", "parallel", "arbitrary"),
            vmem_limit_bytes=_vmem_limit(est)),
        name="flash_attention",
    )(qkv, qkv, qkv)


def _matmul_residual_kernel(a_ref, w_ref, x_ref, o_ref):
    o_ref[...] = x_ref[...] + jnp.dot(a_ref[...], w_ref[...], preferred_element_type=F32)


def _matmul_residual(a, w, x2, *, tm=512):
    t, k = a.shape
    d = w.shape[1]
    est = 2 * tm * k * 2 + 2 * k * d * 2 + 4 * tm * d * 4
    return pl.pallas_call(
        _matmul_residual_kernel,
        grid=(t // tm,),
        in_specs=[
            pl.BlockSpec((tm, k), lambda i: (i, 0)),
            pl.BlockSpec((k, d), lambda i: (0, 0)),
            pl.BlockSpec((tm, d), lambda i: (i, 0)),
        ],
        out_specs=pl.BlockSpec((tm, d), lambda i: (i, 0)),
        out_shape=jax.ShapeDtypeStruct((t, d), F32),
        compiler_params=pltpu.CompilerParams(
            dimension_semantics=("parallel",),
            vmem_limit_bytes=_vmem_limit(est)),
        name="matmul_residual",
    )(a, w, x2)


def _ffn_kernel(x_ref, xp_ref, xq_ref, g_ref, wg_ref, wv_ref, cw_ref, cb_ref, wd_ref,
                *rest, tiles_per_seq, final):
    if final:
        fg_ref, o_ref, xn_ref, gx_ref, acc_ref = rest
    else:
        o_ref, xn_ref, gx_ref, acc_ref = rest
    tm = x_ref.shape[0]
    i = pl.program_id(0)
    j = pl.program_id(1)

    @pl.when(j == 0)
    def _():
        gain = g_ref[...]
        first = (i % tiles_per_seq) == 0
        last = (i % tiles_per_seq) == tiles_per_seq - 1
        xp = jnp.where(first, 0.0, _rms(xp_ref[...], gain))
        xq = jnp.where(last, 0.0, _rms(xq_ref[...], gain))
        xn_ref[:HALO] = xp.astype(BF16)
        xn_ref[HALO:HALO + tm] = _rms(x_ref[...], gain).astype(BF16)
        xn_ref[HALO + tm:] = xq.astype(BF16)
        acc_ref[...] = jnp.zeros(acc_ref.shape, F32)

    gx_ref[...] = jnp.dot(xn_ref[...], wg_ref[...], preferred_element_type=F32)
    v = jnp.dot(xn_ref[HALO:HALO + tm], wv_ref[...], preferred_element_type=F32)
    cw = cw_ref[...]
    g = (cw[0:1] * gx_ref[pl.ds(HALO - 1, tm), :]
         + cw[1:2] * gx_ref[pl.ds(HALO, tm), :]
         + cw[2:3] * gx_ref[pl.ds(HALO + 1, tm), :]
         + cb_ref[...])
    act = (jax.nn.gelu(g) * v).astype(BF16)
    acc_ref[...] += jnp.dot(act, wd_ref[...], preferred_element_type=F32)

    @pl.when(j == pl.num_programs(1) - 1)
    def _():
        y = x_ref[...] + acc_ref[...]
        if final:
            y = _rms(y, fg_ref[...])
        o_ref[...] = y


def _ffn(x2, gain, w_in, conv_w, conv_b, w_down, *, seq, final_gain=None, tm=512, tn=512):
    t, d = x2.shape
    f = w_down.shape[0]
    nj = f // tn
    hb = tm // HALO
    n_hblocks = t // HALO
    final = final_gain is not None
    in_specs = [
        pl.BlockSpec((tm, d), lambda i, j: (i, 0)),
        pl.BlockSpec((HALO, d), lambda i, j: (jnp.maximum(i * hb - 1, 0), 0)),
        pl.BlockSpec((HALO, d), lambda i, j: (jnp.minimum((i + 1) * hb, n_hblocks - 1), 0)),
        pl.BlockSpec((1, d), lambda i, j: (0, 0)),
        pl.BlockSpec((d, tn), lambda i, j: (0, j)),
        pl.BlockSpec((d, tn), lambda i, j: (0, j + nj)),
        pl.BlockSpec((3, tn), lambda i, j: (0, j)),
        pl.BlockSpec((1, tn), lambda i, j: (0, j)),
        pl.BlockSpec((tn, d), lambda i, j: (j, 0)),
    ]
    args = [x2, x2, x2, gain.reshape(1, d), w_in, w_in, conv_w, conv_b.reshape(1, f), w_down]
    if final:
        in_specs.append(pl.BlockSpec((1, d), lambda i, j: (0, 0)))
        args.append(final_gain.reshape(1, d))
    est = (2 * tm * d * 4 + 4 * HALO * d * 4 + (tm + 2 * HALO) * d * 2 + 3 * 2 * d * tn * 2
           + (tm + 2 * HALO) * tn * 4 + tm * d * 4 + 2 * tm * d * 4 + 4 * tm * tn * 4)
    return pl.pallas_call(
        functools.partial(_ffn_kernel, tiles_per_seq=seq // tm, final=final),
        grid=(t // tm, nj),
        in_specs=in_specs,
        out_specs=pl.BlockSpec((tm, d), lambda i, j: (i, 0)),
        out_shape=jax.ShapeDtypeStruct((t, d), F32),
        scratch_shapes=[
            pltpu.VMEM((tm + 2 * HALO, d), BF16),
            pltpu.VMEM((tm + 2 * HALO, tn), F32),
            pltpu.VMEM((tm, d), F32),
        ],
        compiler_params=pltpu.CompilerParams(
            dimension_semantics=("parallel", "arbitrary"),
            vmem_limit_bytes=_vmem_limit(est)),
        name="conv_ffn_final" if final else "conv_ffn",
    )(*args)


def _sigmoid(x):
    return 0.5 * jnp.tanh(0.5 * x) + 0.5


def _softplus(x):
    return jnp.maximum(x, 0.0) + jnp.log1p(jnp.exp(-jnp.abs(x)))


def _group_scan(a, b, reverse):
    sub = lax.broadcasted_iota(jnp.int32, (1, 8, a.shape[-1]), 1)
    for dist in (1, 2, 4):
        if reverse:
            keep = sub < 8 - dist
            shift = 8 - dist
        else:
            keep = sub >= dist
            shift = dist
        a_s = jnp.where(keep, pltpu.roll(a, shift, 1), 1.0)
        b_s = jnp.where(keep, pltpu.roll(b, shift, 1), 0.0)
        b = a * b_s + b
        a = a * a_s
    return a, b


def _chunk_scan(a, b, h, reverse):
    tc, c = a.shape
    groups = tc // 8
    a3, b3 = _group_scan(a.reshape(groups, 8, c), b.reshape(groups, 8, c), reverse)
    outs = [None] * groups
    order = range(groups - 1, -1, -1) if reverse else range(groups)
    for g in order:
        hg = b3[g] + a3[g] * h
        outs[g] = hg
        h = hg[0:1] if reverse else hg[7:8]
    return jnp.concatenate(outs, axis=0), h


def _rec_kernel(xr_ref, gt_ref, cw_ref, cb_ref, wg_ref, ba_ref, bi_ref, lam_ref,
                y_ref, u_ref, hf_ref, *, tc):
    seq = xr_ref.shape[1]
    nc = seq // tc
    c = xr_ref.shape[2]
    cw = cw_ref[...]
    cb = cb_ref[...]
    sp = _softplus(-lam_ref[...])
    win_rows = tc + 2 * HALO

    def conv(ci):
        r0 = pl.multiple_of(ci * tc, tc)
        rp = pl.multiple_of(jnp.maximum(r0 - HALO, 0), HALO)
        rn = pl.multiple_of(jnp.minimum(r0 + tc, seq - HALO), HALO)
        prev = jnp.where(ci == 0, 0.0, xr_ref[0, pl.ds(rp, HALO), :].astype(F32))
        nxt = jnp.where(ci == nc - 1, 0.0, xr_ref[0, pl.ds(rn, HALO), :].astype(F32))
        main = xr_ref[0, pl.ds(r0, tc), :].astype(F32)
        win = jnp.concatenate([prev, main, nxt], axis=0)

        def shifted(k):
            if k == 0:
                return main
            return pltpu.roll(win, (-k) % win_rows, 0)[HALO:HALO + tc]

        return (cw[0:1] * shifted(-2) + cw[1:2] * shifted(-1) + cw[2:3] * main
                + cw[3:4] * shifted(1) + cb)

    def gates(u, e):
        z = jnp.dot(u.astype(BF16), wg_ref[e, 0], preferred_element_type=F32)
        r = _sigmoid(z[:, :c] + ba_ref[e:e + 1, :])
        ig = _sigmoid(z[:, c:] + bi_ref[e:e + 1, :])
        log_a = (-LRU_C) * r * sp[e:e + 1]
        a = jnp.exp(log_a)
        bt = jnp.sqrt(-jnp.tanh(log_a) * (a * a + 1.0)) * (ig * u)
        return a, bt

    def fwd(ci, h):
        r0 = pl.multiple_of(ci * tc, tc)
        u = conv(ci)
        u_ref[pl.ds(r0, tc), :] = u
        a, bt = gates(u, 0)
        hs, h = _chunk_scan(a, bt, h, reverse=False)
        hf_ref[pl.ds(r0, tc), :] = hs
        return h

    def bwd(k, h):
        ci = nc - 1 - k
        r0 = pl.multiple_of(ci * tc, tc)
        u = u_ref[pl.ds(r0, tc), :]
        a, bt = gates(u, 1)
        hs, h = _chunk_scan(a, bt, h, reverse=True)
        gate = gt_ref[0, pl.ds(r0, tc), :].astype(F32)
        y = (hf_ref[pl.ds(r0, tc), :] + hs) * jax.nn.gelu(gate)
        y_ref[0, pl.ds(r0, tc), :] = y.astype(y_ref.dtype)
        return h

    h0 = jnp.zeros((1, c), F32)
    lax.fori_loop(0, nc, fwd, h0)
    lax.fori_loop(0, nc, bwd, h0)


def _rglru(xg, conv_w, conv_b, w_gates, b_a, b_i, lam, *, tc=256):
    b, s, _ = xg.shape
    c = RNN_BLOCK
    est = 2 * 2 * s * c * 2 + 2 * s * c * 2 + 2 * s * c * 4 + 2 * 2 * c * 2 * c * 2 + 64 * tc * c * 4
    return pl.pallas_call(
        functools.partial(_rec_kernel, tc=tc),
        grid=(b, N_RNN_BLOCKS),
        in_specs=[
            pl.BlockSpec((1, s, c), lambda bi, n: (bi, 0, n)),
            pl.BlockSpec((1, s, c), lambda bi, n: (bi, 0, N_RNN_BLOCKS + n)),
            pl.BlockSpec((REC_CONV_W, c), lambda bi, n: (0, n)),
            pl.BlockSpec((1, c), lambda bi, n: (0, n)),
            pl.BlockSpec((2, 1, c, 2 * c), lambda bi, n: (0, n, 0, 0)),
            pl.BlockSpec((2, c), lambda bi, n: (0, n)),
            pl.BlockSpec((2, c), lambda bi, n: (0, n)),
            pl.BlockSpec((2, c), lambda bi, n: (0, n)),
        ],
        out_specs=pl.BlockSpec((1, s, c), lambda bi, n: (bi, 0, n)),
        out_shape=jax.ShapeDtypeStruct((b, s, D_RNN), BF16),
        scratch_shapes=[pltpu.VMEM((s, c), F32), pltpu.VMEM((s, c), F32)],
        compiler_params=pltpu.CompilerParams(
            dimension_semantics=("parallel", "parallel"),
            vmem_limit_bytes=_vmem_limit(est)),
        name="rglru",
    )(xg, xg, conv_w, conv_b.reshape(1, D_RNN), w_gates, b_a, b_i, lam)


def _trunk(x, p):
    b, s, d = x.shape
    t = b * s
    x2 = x.reshape(t, d)
    cos, sin = _rope_tables(s)

    qkv = _norm_matmul(x2, p["norm_mix"][0], p["w_qkv"], seq=s,
                       rope_args=(p["q_gain"], p["k_gain"], cos, sin))
    attn = _attention(qkv.reshape(b, s, QKV_DIM))
    x2 = _matmul_residual(attn.reshape(t, N_HEADS * HEAD_DIM), p["w_o"], x2)
    x2 = _ffn(x2, p["norm_ffn"][0], p["ffn_w_in"][0], p["ffn_conv_w"][0], p["ffn_conv_b"][0],
              p["ffn_w_down"][0], seq=s)

    xg = _norm_matmul(x2, p["norm_mix"][1], p["rec_w_in"], seq=s)
    y = _rglru(xg.reshape(b, s, 2 * D_RNN), p["rec_conv_w"], p["rec_conv_b"], p["rec_w_gates"],
               p["rec_b_a"], p["rec_b_i"], p["rec_lambda"])
    x2 = _matmul_residual(y.reshape(t, D_RNN), p["rec_w_out"], x2)
    x2 = _ffn(x2, p["norm_ffn"][1], p["ffn_w_in"][1], p["ffn_conv_w"][1], p["ffn_conv_b"][1],
              p["ffn_w_down"][1], seq=s, final_gain=p["norm_final"])
    return x2.reshape(b, s, d)


def kernel(x_prompt, x_sample, norm_mix, norm_ffn, norm_final, attn_w_qkv, attn_q_gain, attn_k_gain, attn_w_o, rec_w_in, rec_conv_w, rec_conv_b, rec_w_a, rec_b_a, rec_w_i, rec_b_i, rec_lambda, rec_w_out, ffn_w_in, ffn_conv_w, ffn_conv_b, ffn_w_down):
    assert norm_mix.shape[0] == 2 and attn_w_qkv.shape[0] == 1 and rec_w_in.shape[0] == 1
    p = {
        "norm_mix": norm_mix, "norm_ffn": norm_ffn, "norm_final": norm_final,
        "w_qkv": attn_w_qkv[0].astype(BF16),
        "q_gain": attn_q_gain[0], "k_gain": attn_k_gain[0],
        "w_o": attn_w_o[0].astype(BF16),
        "rec_w_in": rec_w_in[0].astype(BF16),
        "rec_conv_w": rec_conv_w[0], "rec_conv_b": rec_conv_b[0],
        "rec_w_gates": jnp.concatenate([rec_w_a[0], rec_w_i[0]], axis=-1).astype(BF16),
        "rec_b_a": rec_b_a[0], "rec_b_i": rec_b_i[0], "rec_lambda": rec_lambda[0],
        "rec_w_out": rec_w_out[0].astype(BF16),
        "ffn_w_in": ffn_w_in.astype(BF16), "ffn_conv_w": ffn_conv_w, "ffn_conv_b": ffn_conv_b,
        "ffn_w_down": ffn_w_down.astype(BF16),
    }
    return (_trunk(x_prompt, p), _trunk(x_sample, p))
```

```python
import functools

import jax
import jax.numpy as jnp
from jax import lax
from jax.experimental import pallas as pl
from jax.experimental.pallas import tpu as pltpu

D_MODEL = 2048
N_HEADS = 16
N_KV_HEADS = 8
HEAD_DIM = 128
KV_GROUP = N_HEADS // N_KV_HEADS
QKV_DIM = (N_HEADS + 2 * N_KV_HEADS) * HEAD_DIM
ROPE_THETA = 10000.0
GRID_W = 64
AXIS_DIM = HEAD_DIM // 2
ROT_HALF = AXIS_DIM // 2
D_RNN = D_MODEL
N_RNN_BLOCKS = 16
RNN_BLOCK = D_RNN // N_RNN_BLOCKS
REC_CONV_W = 4
LRU_C = 8.0
D_FF = 5632
NORM_EPS = 1e-6
LOG2_E = 1.4426950408889634

BF16_SUBLANES = 16
HALO = BF16_SUBLANES
MIB = 1024 * 1024

F32 = jnp.float32
BF16 = jnp.bfloat16


def _vmem_limit(estimate_bytes):
    return int(min(estimate_bytes * 1.25 + 8 * MIB, 60 * MIB))


def _rms(x, gain):
    ms = jnp.mean(x * x, axis=-1, keepdims=True)
    return x * lax.rsqrt(ms + NORM_EPS) * gain


def _norm_matmul_kernel(x_ref, g_ref, w_ref, *rest, rope, tn):
    if rope:
        qg_ref, kg_ref, cos_ref, sin_ref, o_ref, xn_ref = rest
    else:
        o_ref, xn_ref = rest
    j = pl.program_id(1)

    @pl.when(j == 0)
    def _():
        xn_ref[...] = _rms(x_ref[...], g_ref[...]).astype(BF16)

    y = jnp.dot(xn_ref[...], w_ref[...], preferred_element_type=F32)

    if not rope:
        o_ref[...] = y.astype(o_ref.dtype)
        return

    q_tiles = N_HEADS * HEAD_DIM // tn
    k_tiles = N_KV_HEADS * HEAD_DIM // tn

    def head_norm_rope(gain, post_scale):
        cos = cos_ref[...]
        sin = sin_ref[...]
        lane = lax.broadcasted_iota(jnp.int32, (1, HEAD_DIM), 1)
        low = (lane % AXIS_DIM) < ROT_HALF
        for h in range(tn // HEAD_DIM):
            yh = _rms(y[:, h * HEAD_DIM:(h + 1) * HEAD_DIM], gain)
            partner = jnp.where(low, pltpu.roll(yh, HEAD_DIM - ROT_HALF, 1),
                                pltpu.roll(yh, ROT_HALF, 1))
            yh = yh * cos + partner * sin
            if post_scale is not None:
                yh = yh * post_scale
            o_ref[:, h * HEAD_DIM:(h + 1) * HEAD_DIM] = yh.astype(o_ref.dtype)

    @pl.when(j < q_tiles)
    def _():
        head_norm_rope(qg_ref[...], HEAD_DIM ** -0.5 * LOG2_E)

    @pl.when(jnp.logical_and(j >= q_tiles, j < q_tiles + k_tiles))
    def _():
        head_norm_rope(kg_ref[...], None)

    @pl.when(j >= q_tiles + k_tiles)
    def _():
        o_ref[...] = y.astype(o_ref.dtype)


def _norm_matmul(x2, gain, w, *, seq, rope_args=None, tm=512, tn=512):
    t, d = x2.shape
    n = w.shape[1]
    rope = rope_args is not None
    in_specs = [
        pl.BlockSpec((tm, d), lambda i, j: (i, 0)),
        pl.BlockSpec((1, d), lambda i, j: (0, 0)),
        pl.BlockSpec((d, tn), lambda i, j: (0, j)),
    ]
    args = [x2, gain.reshape(1, d), w]
    if rope:
        q_gain, k_gain, cos, sin = rope_args
        tiles_per_seq = seq // tm
        in_specs += [
            pl.BlockSpec((1, HEAD_DIM), lambda i, j: (0, 0)),
            pl.BlockSpec((1, HEAD_DIM), lambda i, j: (0, 0)),
            pl.BlockSpec((tm, HEAD_DIM), lambda i, j: (i % tiles_per_seq, 0)),
            pl.BlockSpec((tm, HEAD_DIM), lambda i, j: (i % tiles_per_seq, 0)),
        ]
        args += [q_gain.reshape(1, HEAD_DIM), k_gain.reshape(1, HEAD_DIM), cos, sin]
    est = (2 * tm * d * 4 + tm * d * 2 + 2 * d * tn * 2 + 2 * tm * tn * 2
           + 4 * tm * HEAD_DIM * 4 + 2 * tm * tn * 4)
    return pl.pallas_call(
        functools.partial(_norm_matmul_kernel, rope=rope, tn=tn),
        grid=(t // tm, n // tn),
        in_specs=in_specs,
        out_specs=pl.BlockSpec((tm, tn), lambda i, j: (i, j)),
        out_shape=jax.ShapeDtypeStruct((t, n), BF16),
        scratch_shapes=[pltpu.VMEM((tm, d), BF16)],
        compiler_params=pltpu.CompilerParams(
            dimension_semantics=("parallel", "arbitrary"),
            vmem_limit_bytes=_vmem_limit(est)),
        name="norm_matmul_rope" if rope else "norm_matmul",
    )(*args)


def _rope_tables(seq):
    pos = jnp.arange(seq, dtype=jnp.int32)
    row = (pos // GRID_W).astype(F32)
    col = (pos % GRID_W).astype(F32)
    inv = ROPE_THETA ** (-jnp.arange(0, AXIS_DIM, 2, dtype=F32) / AXIS_DIM)
    ang_r = row[:, None] * inv[None, :]
    ang_c = col[:, None] * inv[None, :]
    cos = jnp.concatenate([jnp.cos(ang_r)] * 2 + [jnp.cos(ang_c)] * 2, axis=-1)
    sin = jnp.concatenate([-jnp.sin(ang_r), jnp.sin(ang_r),
                           -jnp.sin(ang_c), jnp.sin(ang_c)], axis=-1)
    return cos, sin


def _attn_kernel(q_ref, k_ref, v_ref, o_ref, q2_ref, v1_ref, m_ref, acc_ref,
                 s_a, s_b, p_a, p_b, al_a, al_b, mx_a, mx_b, *, tk, rb):
    tq = q_ref.shape[1]
    rows2 = KV_GROUP * tq
    seq = k_ref.shape[1]
    n = seq // tk
    lanes = HEAD_DIM

    @pl.when(pl.program_id(2) == 0)
    def _():
        v1_ref[:, :HEAD_DIM] = v_ref[0]
        v1_ref[:, HEAD_DIM:] = jnp.ones((seq, HEAD_DIM), BF16)

    q2_ref[:tq] = q_ref[0, :, :HEAD_DIM]
    q2_ref[tq:] = q_ref[0, :, HEAD_DIM:]
    m_ref[...] = jnp.full(m_ref.shape, -jnp.inf, F32)
    acc_ref[...] = jnp.zeros(acc_ref.shape, F32)

    def scores(j, s_ref, mx_ref):
        cw = 2 * lanes
        for c in range(tk // cw):
            r = pl.multiple_of(j * tk + c * cw, cw)
            s = lax.dot_general(q2_ref[...], k_ref[0, pl.ds(r, cw), :],
                                (((1,), (1,)), ((), ())), preferred_element_type=F32)
            s_ref[:, c * cw:(c + 1) * cw] = s
            mx = jnp.maximum(s[:, :lanes], s[:, lanes:])
            mx_ref[...] = mx if c == 0 else jnp.maximum(mx_ref[...], mx)

    def softmax(s_ref, mx_ref, p_ref, al_ref):
        for b0 in range(0, rows2, rb):
            rows = slice(b0, b0 + rb)
            m_prev = m_ref[rows, :]
            m_new = jnp.maximum(m_prev, jnp.max(mx_ref[rows, :], axis=-1, keepdims=True))
            for t in range(tk // lanes):
                tile = s_ref[rows, t * lanes:(t + 1) * lanes]
                p_ref[rows, t * lanes:(t + 1) * lanes] = jnp.exp2((tile - m_new).astype(BF16))
            m_ref[rows, :] = m_new
            al_ref[rows, :] = jnp.exp2(m_prev - m_new)

    def weighted_values(j, p_ref, al_ref):
        r = pl.multiple_of(j * tk, tk)
        pv = jnp.dot(p_ref[...], v1_ref[pl.ds(r, tk), :], preferred_element_type=F32)
        alpha = al_ref[...]
        acc_ref[...] = jnp.concatenate([alpha, alpha], axis=1) * acc_ref[...] + pv

    scores(0, s_a, mx_a)
    scores(1, s_b, mx_b)
    softmax(s_a, mx_a, p_a, al_a)
    softmax(s_b, mx_b, p_b, al_b)
    if n > 2:
        scores(2, s_a, mx_a)
        scores(3, s_b, mx_b)

    def pair(t, carry):
        j = 2 * t
        weighted_values(j - 2, p_a, al_a)
        weighted_values(j - 1, p_b, al_b)
        softmax(s_a, mx_a, p_a, al_a)
        softmax(s_b, mx_b, p_b, al_b)
        scores(j + 2, s_a, mx_a)
        scores(j + 3, s_b, mx_b)
        return carry

    lax.fori_loop(1, n // 2 - 1, pair, 0)
    if n > 2:
        weighted_values(n - 4, p_a, al_a)
        weighted_values(n - 3, p_b, al_b)
        softmax(s_a, mx_a, p_a, al_a)
        softmax(s_b, mx_b, p_b, al_b)
    weighted_values(n - 2, p_a, al_a)
    weighted_values(n - 1, p_b, al_b)

    o = acc_ref[:, :HEAD_DIM] / acc_ref[:, HEAD_DIM:]
    o_ref[0, :, :HEAD_DIM] = o[:tq].astype(o_ref.dtype)
    o_ref[0, :, HEAD_DIM:] = o[tq:].astype(o_ref.dtype)


def _attention(qkv, *, tq=256, tk=1024, rb=16):
    b, s, _ = qkv.shape
    assert s % (2 * tk) == 0
    gw = KV_GROUP * HEAD_DIM
    rows2 = KV_GROUP * tq
    k_col0 = N_HEADS
    v_col0 = N_HEADS + N_KV_HEADS
    est = (2 * tq * gw * 2 * 2 + 2 * 2 * s * HEAD_DIM * 2 + s * 2 * HEAD_DIM * 2
           + rows2 * HEAD_DIM * (2 + 5 * 4) + 2 * rows2 * tk * (4 + 2))
    return pl.pallas_call(
        functools.partial(_attn_kernel, tk=tk, rb=rb),
        grid=(b, N_KV_HEADS, s // tq),
        in_specs=[
            pl.BlockSpec((1, tq, gw), lambda bi, h, i: (bi, i, h)),
            pl.BlockSpec((1, s, HEAD_DIM), lambda bi, h, i: (bi, 0, k_col0 + h)),
            pl.BlockSpec((1, s, HEAD_DIM), lambda bi, h, i: (bi, 0, v_col0 + h)),
        ],
        out_specs=pl.BlockSpec((1, tq, gw), lambda bi, h, i: (bi, i, h)),
        out_shape=jax.ShapeDtypeStruct((b, s, N_HEADS * HEAD_DIM), BF16),
        scratch_shapes=[
            pltpu.VMEM((rows2, HEAD_DIM), BF16),
            pltpu.VMEM((s, 2 * HEAD_DIM), BF16),
            pltpu.VMEM((rows2, HEAD_DIM), F32),
            pltpu.VMEM((rows2, 2 * HEAD_DIM), F32),
            pltpu.VMEM((rows2, tk), F32),
            pltpu.VMEM((rows2, tk), F32),
            pltpu.VMEM((rows2, tk), BF16),
            pltpu.VMEM((rows2, tk), BF16),
            pltpu.VMEM((rows2, HEAD_DIM), F32),
            pltpu.VMEM((rows2, HEAD_DIM), F32),
            pltpu.VMEM((rows2, HEAD_DIM), F32),
            pltpu.VMEM((rows2, HEAD_DIM), F32),
        ],
        compiler_params=pltpu.CompilerParams(
            dimension_semantics=("parallel", "parallel", "arbitrary"),
            vmem_limit_bytes=_vmem_limit(est)),
        name="flash_attention",
    )(qkv, qkv, qkv)


def _matmul_residual_kernel(a_ref, w_ref, x_ref, o_ref):
    o_ref[...] = x_ref[...] + jnp.dot(a_ref[...], w_ref[...], preferred_element_type=F32)


def _matmul_residual(a, w, x2, *, tm=512):
    t, k = a.shape
    d = w.shape[1]
    est = 2 * tm * k * 2 + 2 * k * d * 2 + 4 * tm * d * 4
    return pl.pallas_call(
        _matmul_residual_kernel,
        grid=(t // tm,),
        in_specs=[
            pl.BlockSpec((tm, k), lambda i: (i, 0)),
            pl.BlockSpec((k, d), lambda i: (0, 0)),
            pl.BlockSpec((tm, d), lambda i: (i, 0)),
        ],
        out_specs=pl.BlockSpec((tm, d), lambda i: (i, 0)),
        out_shape=jax.ShapeDtypeStruct((t, d), F32),
        compiler_params=pltpu.CompilerParams(
            dimension_semantics=("parallel",),
            vmem_limit_bytes=_vmem_limit(est)),
        name="matmul_residual",
    )(a, w, x2)


def _ffn_kernel(x_ref, xp_ref, xq_ref, g_ref, wg_ref, wv_ref, cw_ref, cb_ref, wd_ref,
                *rest, tiles_per_seq, final):
    if final:
        fg_ref, o_ref, xn_ref, gx_ref, acc_ref = rest
    else:
        o_ref, xn_ref, gx_ref, acc_ref = rest
    tm = x_ref.shape[0]
    i = pl.program_id(0)
    j = pl.program_id(1)

    @pl.when(j == 0)
    def _():
        gain = g_ref[...]
        first = (i % tiles_per_seq) == 0
        last = (i % tiles_per_seq) == tiles_per_seq - 1
        xp = jnp.where(first, 0.0, _rms(xp_ref[...], gain))
        xq = jnp.where(last, 0.0, _rms(xq_ref[...], gain))
        xn_ref[:HALO] = xp.astype(BF16)
        xn_ref[HALO:HALO + tm] = _rms(x_ref[...], gain).astype(BF16)
        xn_ref[HALO + tm:] = xq.astype(BF16)
        acc_ref[...] = jnp.zeros(acc_ref.shape, F32)

    gx_ref[...] = jnp.dot(xn_ref[...], wg_ref[...], preferred_element_type=F32)
    v = jnp.dot(xn_ref[HALO:HALO + tm], wv_ref[...], preferred_element_type=F32)
    cw = cw_ref[...]
    g = (cw[0:1] * gx_ref[pl.ds(HALO - 1, tm), :]
         + cw[1:2] * gx_ref[pl.ds(HALO, tm), :]
         + cw[2:3] * gx_ref[pl.ds(HALO + 1, tm), :]
         + cb_ref[...])
    act = (jax.nn.gelu(g) * v).astype(BF16)
    acc_ref[...] += jnp.dot(act, wd_ref[...], preferred_element_type=F32)

    @pl.when(j == pl.num_programs(1) - 1)
    def _():
        y = x_ref[...] + acc_ref[...]
        if final:
            y = _rms(y, fg_ref[...])
        o_ref[...] = y


def _ffn(x2, gain, w_in, conv_w, conv_b, w_down, *, layer, seq, final_gain=None, tm=512, tn=512):
    t, d = x2.shape
    f = w_down.shape[1]
    nj = f // tn
    hb = tm // HALO
    n_hblocks = t // HALO
    final = final_gain is not None
    in_specs = [
        pl.BlockSpec((tm, d), lambda i, j: (i, 0)),
        pl.BlockSpec((HALO, d), lambda i, j: (jnp.maximum(i * hb - 1, 0), 0)),
        pl.BlockSpec((HALO, d), lambda i, j: (jnp.minimum((i + 1) * hb, n_hblocks - 1), 0)),
        pl.BlockSpec((1, d), lambda i, j: (0, 0)),
        pl.BlockSpec((None, d, tn), lambda i, j: (layer, 0, j)),
        pl.BlockSpec((None, d, tn), lambda i, j: (layer, 0, j + nj)),
        pl.BlockSpec((3, tn), lambda i, j: (0, j)),
        pl.BlockSpec((1, tn), lambda i, j: (0, j)),
        pl.BlockSpec((None, tn, d), lambda i, j: (layer, j, 0)),
    ]
    args = [x2, x2, x2, gain.reshape(1, d), w_in, w_in, conv_w, conv_b.reshape(1, f), w_down]
    if final:
        in_specs.append(pl.BlockSpec((1, d), lambda i, j: (0, 0)))
        args.append(final_gain.reshape(1, d))
    est = (2 * tm * d * 4 + 4 * HALO * d * 4 + (tm + 2 * HALO) * d * 2 + 3 * 2 * d * tn * 2
           + (tm + 2 * HALO) * tn * 4 + tm * d * 4 + 2 * tm * d * 4 + 4 * tm * tn * 4)
    return pl.pallas_call(
        functools.partial(_ffn_kernel, tiles_per_seq=seq // tm, final=final),
        grid=(t // tm, nj),
        in_specs=in_specs,
        out_specs=pl.BlockSpec((tm, d), lambda i, j: (i, 0)),
        out_shape=jax.ShapeDtypeStruct((t, d), F32),
        scratch_shapes=[
            pltpu.VMEM((tm + 2 * HALO, d), BF16),
            pltpu.VMEM((tm + 2 * HALO, tn), F32),
            pltpu.VMEM((tm, d), F32),
        ],
        compiler_params=pltpu.CompilerParams(
            dimension_semantics=("parallel", "arbitrary"),
            vmem_limit_bytes=_vmem_limit(est)),
        name="conv_ffn_final" if final else "conv_ffn",
    )(*args)


def _softplus(x):
    return jnp.maximum(x, 0.0) + jnp.log1p(jnp.exp(-jnp.abs(x)))


def _group_scan(a, b, reverse):
    sub = lax.broadcasted_iota(jnp.int32, (1, 8, a.shape[-1]), 1)
    for dist in (1, 2, 4):
        if reverse:
            keep = sub < 8 - dist
            shift = 8 - dist
        else:
            keep = sub >= dist
            shift = dist
        a_s = jnp.where(keep, pltpu.roll(a, shift, 1), 1.0)
        b_s = jnp.where(keep, pltpu.roll(b, shift, 1), 0.0)
        b = a * b_s + b
        a = a * a_s
    return a, b


def _chunk_scan(a, b, h, reverse):
    tc, c = a.shape
    groups = tc // 8
    a3, b3 = _group_scan(a.reshape(groups, 8, c), b.reshape(groups, 8, c), reverse)
    outs = [None] * groups
    order = range(groups - 1, -1, -1) if reverse else range(groups)
    for g in order:
        hg = b3[g] + a3[g] * h
        outs[g] = hg
        h = hg[0:1] if reverse else hg[7:8]
    return jnp.concatenate(outs, axis=0), h


def _rec_kernel(xr_ref, gt_ref, cw_ref, cb_ref, wg_ref, ba_ref, bi_ref, lam_ref,
                y_ref, u_ref, hf_ref, *, tc):
    seq = xr_ref.shape[1]
    nc = seq // tc
    c = xr_ref.shape[2]
    cw = cw_ref[...]
    cb = cb_ref[...]
    half_rate = (-0.5 * LRU_C) * _softplus(-lam_ref[...])
    win_rows = tc + 2 * HALO

    def conv(ci):
        r0 = pl.multiple_of(ci * tc, tc)
        rp = pl.multiple_of(jnp.maximum(r0 - HALO, 0), HALO)
        rn = pl.multiple_of(jnp.minimum(r0 + tc, seq - HALO), HALO)
        prev = jnp.where(ci == 0, 0.0, xr_ref[0, pl.ds(rp, HALO), :].astype(F32))
        nxt = jnp.where(ci == nc - 1, 0.0, xr_ref[0, pl.ds(rn, HALO), :].astype(F32))
        main = xr_ref[0, pl.ds(r0, tc), :].astype(F32)
        win = jnp.concatenate([prev, main, nxt], axis=0)

        def shifted(k):
            if k == 0:
                return main
            return pltpu.roll(win, (-k) % win_rows, 0)[HALO:HALO + tc]

        return (cw[0:1] * shifted(-2) + cw[1:2] * shifted(-1) + cw[2:3] * main
                + cw[3:4] * shifted(1) + cb)

    def gates(u, e):
        z = jnp.dot(u.astype(BF16), wg_ref[e, 0], preferred_element_type=F32)
        t_r = jnp.tanh(z[:, :c] + ba_ref[e:e + 1, :])
        t_i = jnp.tanh(z[:, c:] + bi_ref[e:e + 1, :])
        log_a = half_rate[e:e + 1] * t_r + half_rate[e:e + 1]
        a = jnp.exp(log_a)
        half_u = 0.5 * u
        bt = jnp.sqrt(-jnp.tanh(log_a) * (a * a + 1.0)) * (half_u * t_i + half_u)
        return a, bt

    def fwd(ci, h):
        r0 = pl.multiple_of(ci * tc, tc)
        u = conv(ci)
        u_ref[pl.ds(r0, tc), :] = u
        a, bt = gates(u, 0)
        hs, h = _chunk_scan(a, bt, h, reverse=False)
        hf_ref[pl.ds(r0, tc), :] = hs
        return h

    def bwd(k, h):
        ci = nc - 1 - k
        r0 = pl.multiple_of(ci * tc, tc)
        u = u_ref[pl.ds(r0, tc), :]
        a, bt = gates(u, 1)
        hs, h = _chunk_scan(a, bt, h, reverse=True)
        gate = gt_ref[0, pl.ds(r0, tc), :].astype(F32)
        y = (hf_ref[pl.ds(r0, tc), :] + hs) * jax.nn.gelu(gate)
        y_ref[0, pl.ds(r0, tc), :] = y.astype(y_ref.dtype)
        return h

    h0 = jnp.zeros((1, c), F32)
    lax.fori_loop(0, nc, fwd, h0)
    lax.fori_loop(0, nc, bwd, h0)


def _rglru(xg, conv_w, conv_b, w_gates, b_a, b_i, lam, *, tc=256):
    b, s, _ = xg.shape
    c = RNN_BLOCK
    est = (2 * 2 * s * c * 2 + 2 * s * c * 2 + 2 * s * c * 4 + 2 * 2 * c * 2 * c * 2
           + 3 * tc * c * 4 + 64 * tc * c * 4)
    return pl.pallas_call(
        functools.partial(_rec_kernel, tc=tc),
        grid=(b, N_RNN_BLOCKS),
        in_specs=[
            pl.BlockSpec((1, s, c), lambda bi, n: (bi, 0, n)),
            pl.BlockSpec((1, s, c), lambda bi, n: (bi, 0, N_RNN_BLOCKS + n)),
            pl.BlockSpec((REC_CONV_W, c), lambda bi, n: (0, n)),
            pl.BlockSpec((1, c), lambda bi, n: (0, n)),
            pl.BlockSpec((2, 1, c, 2 * c), lambda bi, n: (0, n, 0, 0)),
            pl.BlockSpec((2, c), lambda bi, n: (0, n)),
            pl.BlockSpec((2, c), lambda bi, n: (0, n)),
            pl.BlockSpec((2, c), lambda bi, n: (0, n)),
        ],
        out_specs=pl.BlockSpec((1, s, c), lambda bi, n: (bi, 0, n)),
        out_shape=jax.ShapeDtypeStruct((b, s, D_RNN), BF16),
        scratch_shapes=[
            pltpu.VMEM((s, c), F32),
            pltpu.VMEM((s, c), F32),
        ],
        compiler_params=pltpu.CompilerParams(
            dimension_semantics=("parallel", "parallel"),
            vmem_limit_bytes=_vmem_limit(est)),
        name="rglru",
    )(xg, xg, conv_w, conv_b.reshape(1, D_RNN), w_gates, b_a, b_i, lam)


def _trunk(x, p):
    b, s, d = x.shape
    t = b * s
    x2 = x.reshape(t, d)
    cos, sin = _rope_tables(s)

    qkv = _norm_matmul(x2, p["norm_mix"][0], p["w_qkv"], seq=s,
                       rope_args=(p["q_gain"], p["k_gain"], cos, sin))
    attn = _attention(qkv.reshape(b, s, QKV_DIM))
    x2 = _matmul_residual(attn.reshape(t, N_HEADS * HEAD_DIM), p["w_o"], x2)
    x2 = _ffn(x2, p["norm_ffn"][0], p["ffn_w_in"], p["ffn_conv_w"][0], p["ffn_conv_b"][0],
              p["ffn_w_down"], layer=0, seq=s)

    xg = _norm_matmul(x2, p["norm_mix"][1], p["rec_w_in"], seq=s)
    y = _rglru(xg.reshape(b, s, 2 * D_RNN), p["rec_conv_w"], p["rec_conv_b"], p["rec_w_gates"],
               p["rec_b_a"], p["rec_b_i"], p["rec_lambda"])
    x2 = _matmul_residual(y.reshape(t, D_RNN), p["rec_w_out"], x2)
    x2 = _ffn(x2, p["norm_ffn"][1], p["ffn_w_in"], p["ffn_conv_w"][1], p["ffn_conv_b"][1],
              p["ffn_w_down"], layer=1, seq=s, final_gain=p["norm_final"])
    return x2.reshape(b, s, d)


def kernel(x_prompt, x_sample, norm_mix, norm_ffn, norm_final, attn_w_qkv, attn_q_gain, attn_k_gain, attn_w_o, rec_w_in, rec_conv_w, rec_conv_b, rec_w_a, rec_b_a, rec_w_i, rec_b_i, rec_lambda, rec_w_out, ffn_w_in, ffn_conv_w, ffn_conv_b, ffn_w_down):
    assert norm_mix.shape[0] == 2 and attn_w_qkv.shape[0] == 1 and rec_w_in.shape[0] == 1
    p = {
        "norm_mix": norm_mix, "norm_ffn": norm_ffn, "norm_final": norm_final,
        "w_qkv": attn_w_qkv[0].astype(BF16),
        "q_gain": attn_q_gain[0], "k_gain": attn_k_gain[0],
        "w_o": attn_w_o[0].astype(BF16),
        "rec_w_in": rec_w_in[0].astype(BF16),
        "rec_conv_w": rec_conv_w[0], "rec_conv_b": rec_conv_b[0],
        "rec_w_gates": (0.5 * jnp.concatenate([rec_w_a[0], rec_w_i[0]], axis=-1)).astype(BF16),
        "rec_b_a": 0.5 * rec_b_a[0], "rec_b_i": 0.5 * rec_b_i[0], "rec_lambda": rec_lambda[0],
        "rec_w_out": rec_w_out[0].astype(BF16),
        "ffn_w_in": ffn_w_in.astype(BF16), "ffn_conv_w": ffn_conv_w, "ffn_conv_b": ffn_conv_b,
        "ffn_w_down": ffn_w_down.astype(BF16),
    }
    return (_trunk(x_prompt, p), _trunk(x_sample, p))
```

```python
import functools

import jax
import jax.numpy as jnp
from jax import lax
from jax.experimental import pallas as pl
from jax.experimental.pallas import tpu as pltpu

D_MODEL = 2048
N_HEADS = 16
N_KV_HEADS = 8
HEAD_DIM = 128
KV_GROUP = N_HEADS // N_KV_HEADS
QKV_DIM = (N_HEADS + 2 * N_KV_HEADS) * HEAD_DIM
ROPE_THETA = 10000.0
GRID_W = 64
AXIS_DIM = HEAD_DIM // 2
ROT_HALF = AXIS_DIM // 2
D_RNN = D_MODEL
N_RNN_BLOCKS = 16
RNN_BLOCK = D_RNN // N_RNN_BLOCKS
REC_CONV_W = 4
LRU_C = 8.0
D_FF = 5632
NORM_EPS = 1e-6
LOG2_E = 1.4426950408889634

BF16_SUBLANES = 16
HALO = BF16_SUBLANES
MIB = 1024 * 1024

F32 = jnp.float32
BF16 = jnp.bfloat16


def _vmem_limit(estimate_bytes):
    return int(min(estimate_bytes * 1.25 + 8 * MIB, 60 * MIB))


def _rms(x, gain):
    ms = jnp.mean(x * x, axis=-1, keepdims=True)
    return x * lax.rsqrt(ms + NORM_EPS) * gain


def _norm_matmul_kernel(x_ref, g_ref, w_ref, *rest, rope, cn):
    if rope:
        qg_ref, kg_ref, cos_ref, sin_ref, o_ref = rest
    else:
        (o_ref,) = rest
    xn = _rms(x_ref[...], g_ref[...]).astype(BF16)
    q_cols = N_HEADS * HEAD_DIM
    k_cols = N_KV_HEADS * HEAD_DIM

    def head_norm_rope(y, col0, gain, post_scale):
        cos = cos_ref[...]
        sin = sin_ref[...]
        lane = lax.broadcasted_iota(jnp.int32, (1, HEAD_DIM), 1)
        low = (lane % AXIS_DIM) < ROT_HALF
        for h in range(cn // HEAD_DIM):
            yh = _rms(y[:, h * HEAD_DIM:(h + 1) * HEAD_DIM], gain)
            partner = jnp.where(low, pltpu.roll(yh, HEAD_DIM - ROT_HALF, 1),
                                pltpu.roll(yh, ROT_HALF, 1))
            yh = yh * cos + partner * sin
            if post_scale is not None:
                yh = yh * post_scale
            o_ref[:, col0 + h * HEAD_DIM:col0 + (h + 1) * HEAD_DIM] = yh.astype(o_ref.dtype)

    for col0 in range(0, w_ref.shape[1], cn):
        y = jnp.dot(xn, w_ref[:, col0:col0 + cn], preferred_element_type=F32)
        if rope and col0 < q_cols:
            head_norm_rope(y, col0, qg_ref[...], HEAD_DIM ** -0.5 * LOG2_E)
        elif rope and col0 < q_cols + k_cols:
            head_norm_rope(y, col0, kg_ref[...], None)
        else:
            o_ref[:, col0:col0 + cn] = y.astype(o_ref.dtype)


def _norm_matmul(x2, gain, w, *, seq, rope_args=None, tm=512, cn=512):
    t, d = x2.shape
    n = w.shape[1]
    rope = rope_args is not None
    in_specs = [
        pl.BlockSpec((tm, d), lambda i: (i, 0)),
        pl.BlockSpec((1, d), lambda i: (0, 0)),
        pl.BlockSpec((d, n), lambda i: (0, 0), pipeline_mode=pl.Buffered(1)),
    ]
    args = [x2, gain.reshape(1, d), w]
    if rope:
        q_gain, k_gain, cos, sin = rope_args
        tiles_per_seq = seq // tm
        in_specs += [
            pl.BlockSpec((1, HEAD_DIM), lambda i: (0, 0)),
            pl.BlockSpec((1, HEAD_DIM), lambda i: (0, 0)),
            pl.BlockSpec((tm, HEAD_DIM), lambda i: (i % tiles_per_seq, 0)),
            pl.BlockSpec((tm, HEAD_DIM), lambda i: (i % tiles_per_seq, 0)),
        ]
        args += [q_gain.reshape(1, HEAD_DIM), k_gain.reshape(1, HEAD_DIM), cos, sin]
    est = (2 * tm * d * 4 + tm * d * 2 + d * n * 2 + 2 * tm * n * 2
           + 4 * tm * HEAD_DIM * 4 + 4 * tm * cn * 4)
    return pl.pallas_call(
        functools.partial(_norm_matmul_kernel, rope=rope, cn=cn),
        grid=(t // tm,),
        in_specs=in_specs,
        out_specs=pl.BlockSpec((tm, n), lambda i: (i, 0)),
        out_shape=jax.ShapeDtypeStruct((t, n), BF16),
        compiler_params=pltpu.CompilerParams(
            dimension_semantics=("parallel",),
            vmem_limit_bytes=_vmem_limit(est)),
        name="norm_matmul_rope" if rope else "norm_matmul",
    )(*args)


def _rope_tables(seq):
    pos = jnp.arange(seq, dtype=jnp.int32)
    row = (pos // GRID_W).astype(F32)
    col = (pos % GRID_W).astype(F32)
    inv = ROPE_THETA ** (-jnp.arange(0, AXIS_DIM, 2, dtype=F32) / AXIS_DIM)
    ang_r = row[:, None] * inv[None, :]
    ang_c = col[:, None] * inv[None, :]
    cos = jnp.concatenate([jnp.cos(ang_r)] * 2 + [jnp.cos(ang_c)] * 2, axis=-1)
    sin = jnp.concatenate([-jnp.sin(ang_r), jnp.sin(ang_r),
                           -jnp.sin(ang_c), jnp.sin(ang_c)], axis=-1)
    return cos, sin


def _attn_kernel(q_ref, k_ref, v_ref, o_ref, q2_ref, v1_ref, m_ref, acc_ref,
                 s_a, s_b, p_a, p_b, al_a, al_b, mx_a, mx_b, *, tk, rb):
    tq = q_ref.shape[1]
    rows2 = KV_GROUP * tq
    seq = k_ref.shape[1]
    n = seq // tk
    lanes = HEAD_DIM

    @pl.when(pl.program_id(2) == 0)
    def _():
        v1_ref[:, :HEAD_DIM] = v_ref[0]
        v1_ref[:, HEAD_DIM:] = jnp.ones((seq, HEAD_DIM), BF16)

    q2_ref[:tq] = q_ref[0, :, :HEAD_DIM]
    q2_ref[tq:] = q_ref[0, :, HEAD_DIM:]
    m_ref[...] = jnp.full(m_ref.shape, -jnp.inf, F32)
    acc_ref[...] = jnp.zeros(acc_ref.shape, F32)

    def scores(j, s_ref, mx_ref):
        cw = 2 * lanes
        for c in range(tk // cw):
            r = pl.multiple_of(j * tk + c * cw, cw)
            s = lax.dot_general(q2_ref[...], k_ref[0, pl.ds(r, cw), :],
                                (((1,), (1,)), ((), ())), preferred_element_type=F32)
            s_ref[:, c * cw:(c + 1) * cw] = s
            mx = jnp.maximum(s[:, :lanes], s[:, lanes:])
            mx_ref[...] = mx if c == 0 else jnp.maximum(mx_ref[...], mx)

    def softmax(s_ref, mx_ref, p_ref, al_ref):
        for b0 in range(0, rows2, rb):
            rows = slice(b0, b0 + rb)
            m_prev = m_ref[rows, :]
            m_new = jnp.maximum(m_prev, jnp.max(mx_ref[rows, :], axis=-1, keepdims=True))
            for t in range(tk // lanes):
                tile = s_ref[rows, t * lanes:(t + 1) * lanes]
                p_ref[rows, t * lanes:(t + 1) * lanes] = jnp.exp2((tile - m_new).astype(BF16))
            m_ref[rows, :] = m_new
            al_ref[rows, :] = jnp.exp2(m_prev - m_new)

    def weighted_values(j, p_ref, al_ref):
        r = pl.multiple_of(j * tk, tk)
        pv = jnp.dot(p_ref[...], v1_ref[pl.ds(r, tk), :], preferred_element_type=F32)
        alpha = al_ref[...]
        acc_ref[...] = jnp.concatenate([alpha, alpha], axis=1) * acc_ref[...] + pv

    scores(0, s_a, mx_a)
    scores(1, s_b, mx_b)
    softmax(s_a, mx_a, p_a, al_a)
    softmax(s_b, mx_b, p_b, al_b)
    if n > 2:
        scores(2, s_a, mx_a)
        scores(3, s_b, mx_b)

    def pair(t, carry):
        j = 2 * t
        weighted_values(j - 2, p_a, al_a)
        weighted_values(j - 1, p_b, al_b)
        softmax(s_a, mx_a, p_a, al_a)
        softmax(s_b, mx_b, p_b, al_b)
        scores(j + 2, s_a, mx_a)
        scores(j + 3, s_b, mx_b)
        return carry

    lax.fori_loop(1, n // 2 - 1, pair, 0)
    if n > 2:
        weighted_values(n - 4, p_a, al_a)
        weighted_values(n - 3, p_b, al_b)
        softmax(s_a, mx_a, p_a, al_a)
        softmax(s_b, mx_b, p_b, al_b)
    weighted_values(n - 2, p_a, al_a)
    weighted_values(n - 1, p_b, al_b)

    o = acc_ref[:, :HEAD_DIM] / acc_ref[:, HEAD_DIM:]
    o_ref[0, :, :HEAD_DIM] = o[:tq].astype(o_ref.dtype)
    o_ref[0, :, HEAD_DIM:] = o[tq:].astype(o_ref.dtype)


def _attention(qkv, *, tq=512, tk=1024, rb=16):
    b, s, _ = qkv.shape
    assert s % (2 * tk) == 0
    gw = KV_GROUP * HEAD_DIM
    rows2 = KV_GROUP * tq
    k_col0 = N_HEADS
    v_col0 = N_HEADS + N_KV_HEADS
    est = (2 * tq * gw * 2 * 2 + 2 * 2 * s * HEAD_DIM * 2 + s * 2 * HEAD_DIM * 2
           + rows2 * HEAD_DIM * (2 + 5 * 4) + 2 * rows2 * tk * (4 + 2))
    return pl.pallas_call(
        functools.partial(_attn_kernel, tk=tk, rb=rb),
        grid=(b, N_KV_HEADS, s // tq),
        in_specs=[
            pl.BlockSpec((1, tq, gw), lambda bi, h, i: (bi, i, h)),
            pl.BlockSpec((1, s, HEAD_DIM), lambda bi, h, i: (bi, 0, k_col0 + h)),
            pl.BlockSpec((1, s, HEAD_DIM), lambda bi, h, i: (bi, 0, v_col0 + h)),
        ],
        out_specs=pl.BlockSpec((1, tq, gw), lambda bi, h, i: (bi, i, h)),
        out_shape=jax.ShapeDtypeStruct((b, s, N_HEADS * HEAD_DIM), BF16),
        scratch_shapes=[
            pltpu.VMEM((rows2, HEAD_DIM), BF16),
            pltpu.VMEM((s, 2 * HEAD_DIM), BF16),
            pltpu.VMEM((rows2, HEAD_DIM), F32),
            pltpu.VMEM((rows2, 2 * HEAD_DIM), F32),
            pltpu.VMEM((rows2, tk), F32),
            pltpu.VMEM((rows2, tk), F32),
            pltpu.VMEM((rows2, tk), BF16),
            pltpu.VMEM((rows2, tk), BF16),
            pltpu.VMEM((rows2, HEAD_DIM), F32),
            pltpu.VMEM((rows2, HEAD_DIM), F32),
            pltpu.VMEM((rows2, HEAD_DIM), F32),
            pltpu.VMEM((rows2, HEAD_DIM), F32),
        ],
        compiler_params=pltpu.CompilerParams(
            dimension_semantics=("parallel", "parallel", "arbitrary"),
            vmem_limit_bytes=_vmem_limit(est)),
        name="flash_attention",
    )(qkv, qkv, qkv)


def _matmul_residual_kernel(a_ref, w_ref, x_ref, o_ref):
    o_ref[...] = x_ref[...] + jnp.dot(a_ref[...], w_ref[...], preferred_element_type=F32)


def _matmul_residual(a, w, x2, *, tm=512):
    t, k = a.shape
    d = w.shape[1]
    est = 2 * tm * k * 2 + 2 * k * d * 2 + 4 * tm * d * 4
    return pl.pallas_call(
        _matmul_residual_kernel,
        grid=(t // tm,),
        in_specs=[
            pl.BlockSpec((tm, k), lambda i: (i, 0)),
            pl.BlockSpec((k, d), lambda i: (0, 0)),
            pl.BlockSpec((tm, d), lambda i: (i, 0)),
        ],
        out_specs=pl.BlockSpec((tm, d), lambda i: (i, 0)),
        out_shape=jax.ShapeDtypeStruct((t, d), F32),
        compiler_params=pltpu.CompilerParams(
            dimension_semantics=("parallel",),
            vmem_limit_bytes=_vmem_limit(est)),
        name="matmul_residual",
    )(a, w, x2)


def _ffn_kernel(x_ref, xp_ref, xq_ref, g_ref, wg_ref, wv_ref, cw_ref, cb_ref, wd_ref,
                *rest, tiles_per_seq, final):
    if final:
        fg_ref, o_ref, xn_ref, gx_ref, acc_ref = rest
    else:
        o_ref, xn_ref, gx_ref, acc_ref = rest
    tm = x_ref.shape[0]
    i = pl.program_id(0)
    j = pl.program_id(1)

    @pl.when(j == 0)
    def _():
        gain = g_ref[...]
        first = (i % tiles_per_seq) == 0
        last = (i % tiles_per_seq) == tiles_per_seq - 1
        xp = jnp.where(first, 0.0, _rms(xp_ref[...], gain))
        xq = jnp.where(last, 0.0, _rms(xq_ref[...], gain))
        xn_ref[:HALO] = xp.astype(BF16)
        xn_ref[HALO:HALO + tm] = _rms(x_ref[...], gain).astype(BF16)
        xn_ref[HALO + tm:] = xq.astype(BF16)
        acc_ref[...] = jnp.zeros(acc_ref.shape, F32)

    gx_ref[...] = jnp.dot(xn_ref[...], wg_ref[...], preferred_element_type=F32)
    v = jnp.dot(xn_ref[HALO:HALO + tm], wv_ref[...], preferred_element_type=F32)
    cw = cw_ref[...]
    g = (cw[0:1] * gx_ref[pl.ds(HALO - 1, tm), :]
         + cw[1:2] * gx_ref[pl.ds(HALO, tm), :]
         + cw[2:3] * gx_ref[pl.ds(HALO + 1, tm), :]
         + cb_ref[...])
    act = (jax.nn.gelu(g) * v).astype(BF16)
    acc_ref[...] += jnp.dot(act, wd_ref[...], preferred_element_type=F32)

    @pl.when(j == pl.num_programs(1) - 1)
    def _():
        y = x_ref[...] + acc_ref[...]
        if final:
            y = _rms(y, fg_ref[...])
        o_ref[...] = y


def _ffn(x2, gain, w_in, conv_w, conv_b, w_down, *, layer, seq, final_gain=None, tm=512, tn=512):
    t, d = x2.shape
    f = w_down.shape[1]
    nj = f // tn
    hb = tm // HALO
    n_hblocks = t // HALO
    final = final_gain is not None
    in_specs = [
        pl.BlockSpec((tm, d), lambda i, j: (i, 0)),
        pl.BlockSpec((HALO, d), lambda i, j: (jnp.maximum(i * hb - 1, 0), 0)),
        pl.BlockSpec((HALO, d), lambda i, j: (jnp.minimum((i + 1) * hb, n_hblocks - 1), 0)),
        pl.BlockSpec((1, d), lambda i, j: (0, 0)),
        pl.BlockSpec((None, d, tn), lambda i, j: (layer, 0, j)),
        pl.BlockSpec((None, d, tn), lambda i, j: (layer, 0, j + nj)),
        pl.BlockSpec((3, tn), lambda i, j: (0, j)),
        pl.BlockSpec((1, tn), lambda i, j: (0, j)),
        pl.BlockSpec((None, tn, d), lambda i, j: (layer, j, 0)),
    ]
    args = [x2, x2, x2, gain.reshape(1, d), w_in, w_in, conv_w, conv_b.reshape(1, f), w_down]
    if final:
        in_specs.append(pl.BlockSpec((1, d), lambda i, j: (0, 0)))
        args.append(final_gain.reshape(1, d))
    est = (2 * tm * d * 4 + 4 * HALO * d * 4 + (tm + 2 * HALO) * d * 2 + 3 * 2 * d * tn * 2
           + (tm + 2 * HALO) * tn * 4 + tm * d * 4 + 2 * tm * d * 4 + 4 * tm * tn * 4)
    return pl.pallas_call(
        functools.partial(_ffn_kernel, tiles_per_seq=seq // tm, final=final),
        grid=(t // tm, nj),
        in_specs=in_specs,
        out_specs=pl.BlockSpec((tm, d), lambda i, j: (i, 0)),
        out_shape=jax.ShapeDtypeStruct((t, d), F32),
        scratch_shapes=[
            pltpu.VMEM((tm + 2 * HALO, d), BF16),
            pltpu.VMEM((tm + 2 * HALO, tn), F32),
            pltpu.VMEM((tm, d), F32),
        ],
        compiler_params=pltpu.CompilerParams(
            dimension_semantics=("parallel", "arbitrary"),
            vmem_limit_bytes=_vmem_limit(est)),
        name="conv_ffn_final" if final else "conv_ffn",
    )(*args)


def _softplus(x):
    return jnp.maximum(x, 0.0) + jnp.log1p(jnp.exp(-jnp.abs(x)))


def _group_scan(a, b, reverse):
    sub = lax.broadcasted_iota(jnp.int32, (1, 8, a.shape[-1]), 1)
    for dist in (1, 2, 4):
        if reverse:
            keep = sub < 8 - dist
            shift = 8 - dist
        else:
            keep = sub >= dist
            shift = dist
        a_s = jnp.where(keep, pltpu.roll(a, shift, 1), 1.0)
        b_s = jnp.where(keep, pltpu.roll(b, shift, 1), 0.0)
        b = a * b_s + b
        a = a * a_s
    return a, b


def _chunk_scan(a, b, h, reverse):
    tc, c = a.shape
    groups = tc // 8
    a3, b3 = _group_scan(a.reshape(groups, 8, c), b.reshape(groups, 8, c), reverse)
    outs = [None] * groups
    order = range(groups - 1, -1, -1) if reverse else range(groups)
    for g in order:
        hg = b3[g] + a3[g] * h
        outs[g] = hg
        h = hg[0:1] if reverse else hg[7:8]
    return jnp.concatenate(outs, axis=0), h


def _rec_kernel(xr_ref, gt_ref, cw_ref, cb_ref, wg_ref, ba_ref, bi_ref, lam_ref,
                y_ref, u_ref, hf_ref, *, tc):
    seq = xr_ref.shape[1]
    nc = seq // tc
    c = xr_ref.shape[2]
    cw = cw_ref[...]
    cb = cb_ref[...]
    half_rate = (-0.5 * LRU_C) * _softplus(-lam_ref[...])
    win_rows = tc + 2 * HALO

    def conv(ci):
        r0 = pl.multiple_of(ci * tc, tc)
        rp = pl.multiple_of(jnp.maximum(r0 - HALO, 0), HALO)
        rn = pl.multiple_of(jnp.minimum(r0 + tc, seq - HALO), HALO)
        prev = jnp.where(ci == 0, 0.0, xr_ref[0, pl.ds(rp, HALO), :].astype(F32))
        nxt = jnp.where(ci == nc - 1, 0.0, xr_ref[0, pl.ds(rn, HALO), :].astype(F32))
        main = xr_ref[0, pl.ds(r0, tc), :].astype(F32)
        win = jnp.concatenate([prev, main, nxt], axis=0)

        def shifted(k):
            if k == 0:
                return main
            return pltpu.roll(win, (-k) % win_rows, 0)[HALO:HALO + tc]

        return (cw[0:1] * shifted(-2) + cw[1:2] * shifted(-1) + cw[2:3] * main
                + cw[3:4] * shifted(1) + cb)

    def gates(u, e):
        z = jnp.dot(u.astype(BF16), wg_ref[e, 0], preferred_element_type=F32)
        t_r = jnp.tanh(z[:, :c] + ba_ref[e:e + 1, :])
        t_i = jnp.tanh(z[:, c:] + bi_ref[e:e + 1, :])
        log_a = half_rate[e:e + 1] * t_r + half_rate[e:e + 1]
        a = jnp.exp(log_a)
        half_u = 0.5 * u
        bt = jnp.sqrt(-jnp.tanh(log_a) * (a * a + 1.0)) * (half_u * t_i + half_u)
        return a, bt

    def fwd(ci, h):
        r0 = pl.multiple_of(ci * tc, tc)
        u = conv(ci)
        u_ref[pl.ds(r0, tc), :] = u
        a, bt = gates(u, 0)
        hs, h = _chunk_scan(a, bt, h, reverse=False)
        hf_ref[pl.ds(r0, tc), :] = hs
        return h

    def bwd(k, h):
        ci = nc - 1 - k
        r0 = pl.multiple_of(ci * tc, tc)
        u = u_ref[pl.ds(r0, tc), :]
        a, bt = gates(u, 1)
        hs, h = _chunk_scan(a, bt, h, reverse=True)
        gate = gt_ref[0, pl.ds(r0, tc), :].astype(F32)
        y = (hf_ref[pl.ds(r0, tc), :] + hs) * jax.nn.gelu(gate)
        y_ref[0, pl.ds(r0, tc), :] = y.astype(y_ref.dtype)
        return h

    h0 = jnp.zeros((1, c), F32)
    lax.fori_loop(0, nc, fwd, h0)
    lax.fori_loop(0, nc, bwd, h0)


def _rglru(xg, conv_w, conv_b, w_gates, b_a, b_i, lam, *, tc=256):
    b, s, _ = xg.shape
    c = RNN_BLOCK
    est = (2 * 2 * s * c * 2 + 2 * s * c * 2 + 2 * s * c * 4 + 2 * 2 * c * 2 * c * 2
           + 3 * tc * c * 4 + 64 * tc * c * 4)
    return pl.pallas_call(
        functools.partial(_rec_kernel, tc=tc),
        grid=(b, N_RNN_BLOCKS),
        in_specs=[
            pl.BlockSpec((1, s, c), lambda bi, n: (bi, 0, n)),
            pl.BlockSpec((1, s, c), lambda bi, n: (bi, 0, N_RNN_BLOCKS + n)),
            pl.BlockSpec((REC_CONV_W, c), lambda bi, n: (0, n)),
            pl.BlockSpec((1, c), lambda bi, n: (0, n)),
            pl.BlockSpec((2, 1, c, 2 * c), lambda bi, n: (0, n, 0, 0)),
            pl.BlockSpec((2, c), lambda bi, n: (0, n)),
            pl.BlockSpec((2, c), lambda bi, n: (0, n)),
            pl.BlockSpec((2, c), lambda bi, n: (0, n)),
        ],
        out_specs=pl.BlockSpec((1, s, c), lambda bi, n: (bi, 0, n)),
        out_shape=jax.ShapeDtypeStruct((b, s, D_RNN), BF16),
        scratch_shapes=[
            pltpu.VMEM((s, c), F32),
            pltpu.VMEM((s, c), F32),
        ],
        compiler_params=pltpu.CompilerParams(
            dimension_semantics=("parallel", "parallel"),
            vmem_limit_bytes=_vmem_limit(est)),
        name="rglru",
    )(xg, xg, conv_w, conv_b.reshape(1, D_RNN), w_gates, b_a, b_i, lam)


def _trunk(x, p):
    b, s, d = x.shape
    t = b * s
    x2 = x.reshape(t, d)
    cos, sin = _rope_tables(s)

    qkv = _norm_matmul(x2, p["norm_mix"][0], p["w_qkv"], seq=s,
                       rope_args=(p["q_gain"], p["k_gain"], cos, sin))
    attn = _attention(qkv.reshape(b, s, QKV_DIM))
    x2 = _matmul_residual(attn.reshape(t, N_HEADS * HEAD_DIM), p["w_o"], x2)
    x2 = _ffn(x2, p["norm_ffn"][0], p["ffn_w_in"], p["ffn_conv_w"][0], p["ffn_conv_b"][0],
              p["ffn_w_down"], layer=0, seq=s)

    xg = _norm_matmul(x2, p["norm_mix"][1], p["rec_w_in"], seq=s)
    y = _rglru(xg.reshape(b, s, 2 * D_RNN), p["rec_conv_w"], p["rec_conv_b"], p["rec_w_gates"],
               p["rec_b_a"], p["rec_b_i"], p["rec_lambda"])
    x2 = _matmul_residual(y.reshape(t, D_RNN), p["rec_w_out"], x2)
    x2 = _ffn(x2, p["norm_ffn"][1], p["ffn_w_in"], p["ffn_conv_w"][1], p["ffn_conv_b"][1],
              p["ffn_w_down"], layer=1, seq=s, final_gain=p["norm_final"])
    return x2.reshape(b, s, d)


def kernel(x_prompt, x_sample, norm_mix, norm_ffn, norm_final, attn_w_qkv, attn_q_gain, attn_k_gain, attn_w_o, rec_w_in, rec_conv_w, rec_conv_b, rec_w_a, rec_b_a, rec_w_i, rec_b_i, rec_lambda, rec_w_out, ffn_w_in, ffn_conv_w, ffn_conv_b, ffn_w_down):
    assert norm_mix.shape[0] == 2 and attn_w_qkv.shape[0] == 1 and rec_w_in.shape[0] == 1
    p = {
        "norm_mix": norm_mix, "norm_ffn": norm_ffn, "norm_final": norm_final,
        "w_qkv": attn_w_qkv[0].astype(BF16),
        "q_gain": attn_q_gain[0], "k_gain": attn_k_gain[0],
        "w_o": attn_w_o[0].astype(BF16),
        "rec_w_in": rec_w_in[0].astype(BF16),
        "rec_conv_w": rec_conv_w[0], "rec_conv_b": rec_conv_b[0],
        "rec_w_gates": (0.5 * jnp.concatenate([rec_w_a[0], rec_w_i[0]], axis=-1)).astype(BF16),
        "rec_b_a": 0.5 * rec_b_a[0], "rec_b_i": 0.5 * rec_b_i[0], "rec_lambda": rec_lambda[0],
        "rec_w_out": rec_w_out[0].astype(BF16),
        "ffn_w_in": ffn_w_in.astype(BF16), "ffn_conv_w": ffn_conv_w, "ffn_conv_b": ffn_conv_b,
        "ffn_w_down": ffn_w_down.astype(BF16),
    }
    return (_trunk(x_prompt, p), _trunk(x_sample, p))
```
